```python
import math
import jax, jax.numpy as jnp
from jax import lax
import numpy as np

D_MODEL = 2048
BATCH = 2
SEQ = 16384
DEPTH = 2
DEC_BATCH = 16
DEC_SEQ = 32
PAST_LEN = 1024

CHUNK = 64
POOL_WIDTH = D_MODEL // 4
POOL_WINDOWS = (2, 4, 8, 16)
POOL_GROUP = POOL_WIDTH // len(POOL_WINDOWS)
POOL_HIST = max(POOL_WINDOWS) - 1
S5_WIDTH = D_MODEL // 4
S5_GROUP_CH = 16
S5_GROUPS = S5_WIDTH // S5_GROUP_CH
S5_STATE = 64
SB_WIDTH = D_MODEL // 2
SB_HEAD_DIM = 128
SB_HEADS = SB_WIDTH // SB_HEAD_DIM
Q_BLOCK = 128
D_FF = 4 * D_MODEL
IN_WIDTH = POOL_WIDTH + S5_WIDTH + 3 * SB_WIDTH
MIX_WIDTH = POOL_WIDTH + S5_WIDTH + SB_WIDTH
DEEPNORM_ALPHA = (2 * DEPTH) ** 0.25
DEEPNORM_BETA = (8 * DEPTH) ** -0.25
LN_EPS = 1e-5

kernel_name = 'hymba_pool_s5_stickbreak_deepnorm_adaln_step'


def layer_norm(x, g=None, b=None):
    xf = x.astype(jnp.float32)
    mu = xf.mean(-1, keepdims=True)
    var = jnp.square(xf - mu).mean(-1, keepdims=True)
    y = (xf - mu) * lax.rsqrt(var + LN_EPS)
    if g is not None:
        y = y * g.astype(jnp.float32) + b.astype(jnp.float32)
    return y.astype(x.dtype)


def pool_mixer(p, hist, pos0, w_pool, pool_scale):
    f32 = jnp.float32
    B, L, _ = p.shape
    ext = jnp.concatenate([hist.astype(p.dtype), p], axis=1).astype(f32)
    cs = jnp.concatenate([jnp.zeros_like(ext[:, :1]), jnp.cumsum(ext, axis=1)], axis=1)
    pos = pos0 + jnp.arange(L, dtype=jnp.int32)
    outs = []
    for g, w in enumerate(POOL_WINDOWS):
        lo, hi = g * POOL_GROUP, (g + 1) * POOL_GROUP
        win = cs[:, POOL_HIST + 1:POOL_HIST + 1 + L, lo:hi] - cs[:, POOL_HIST + 1 - w:POOL_HIST + 1 - w + L, lo:hi]
        cnt = jnp.minimum(w, pos + 1).astype(f32)[None, :, None]
        mixed = win / cnt - ext[:, POOL_HIST:, lo:hi]
        outs.append(jnp.einsum('blc,cd->bld', mixed, w_pool[g].astype(f32)))
    out = jnp.concatenate(outs, axis=-1) * pool_scale.astype(f32)
    return out.astype(p.dtype), ext[:, -POOL_HIST:].astype(p.dtype)


def _complex_affine_combine(e1, e2):
    a1r, a1i, b1r, b1i = e1
    a2r, a2i, b2r, b2i = e2
    return (a2r * a1r - a2i * a1i,
            a2r * a1i + a2i * a1r,
            a2r * b1r - a2i * b1i + b2r,
            a2r * b1i + a2i * b1r + b2i)


def s5_mixer(u, h0_re, h0_im, a_re, a_im, log_dt, b_re, b_im, c_re, c_im, d_skip, w_glu):
    f32 = jnp.float32
    B, L, _ = u.shape
    ug = u.astype(f32).reshape(B, L, S5_GROUPS, S5_GROUP_CH)
    a_re = a_re.astype(f32)
    a_im = a_im.astype(f32)
    dt = jnp.exp(log_dt.astype(f32))[:, None]
    mag = jnp.exp(a_re * dt)
    ang = a_im * dt
    ab_re, ab_im = mag * jnp.cos(ang), mag * jnp.sin(ang)
    den = a_re * a_re + a_im * a_im
    f_re = ((ab_re - 1.0) * a_re + ab_im * a_im) / den
    f_im = (ab_im * a_re - (ab_re - 1.0) * a_im) / den
    b_re = b_re.astype(f32)
    b_im = b_im.astype(f32)
    bb_re = f_re[..., None] * b_re - f_im[..., None] * b_im
    bb_im = f_re[..., None] * b_im + f_im[..., None] * b_re
    bu_re = jnp.einsum('blgh,gph->blgp', ug, bb_re)
    bu_im = jnp.einsum('blgh,gph->blgp', ug, bb_im)
    h0r = h0_re.astype(f32)
    h0i = h0_im.astype(f32)
    bu_re = bu_re.at[:, 0].add(ab_re * h0r - ab_im * h0i)
    bu_im = bu_im.at[:, 0].add(ab_re * h0i + ab_im * h0r)
    ar = jnp.broadcast_to(ab_re, bu_re.shape)
    ai = jnp.broadcast_to(ab_im, bu_re.shape)
    _, _, h_re, h_im = lax.associative_scan(_complex_affine_combine, (ar, ai, bu_re, bu_im), axis=1)
    y = (jnp.einsum('blgp,ghp->blgh', h_re, c_re.astype(f32))
         - jnp.einsum('blgp,ghp->blgh', h_im, c_im.astype(f32))
         + d_skip.astype(f32) * ug)
    y = jax.nn.gelu(y.reshape(B, L, S5_WIDTH))
    y = y * jax.nn.sigmoid(y @ w_glu.astype(f32))
    return y.astype(u.dtype), h_re[:, -1].astype(h0_re.dtype), h_im[:, -1].astype(h0_im.dtype)


def stick_breaking_prompt(q, k, v):
    f32 = jnp.float32
    B, L, H, Dh = q.shape
    bs = min(Q_BLOCK, L)
    nb = L // bs
    scale = Dh ** -0.5
    idx = jnp.arange(bs, dtype=jnp.int32)
    incl_mat = (idx[:, None] >= idx[None, :]).astype(f32)
    outs = []
    for i in range(nb):
        n_k = (i + 1) * bs
        qi = q[:, i * bs:(i + 1) * bs]
        ki = k[:, :n_k]
        vi = v[:, :n_k]
        z = jnp.einsum('bqhd,bkhd->bhqk', qi, ki).astype(f32) * scale
        mask = jnp.arange(n_k, dtype=jnp.int32)[None, :] < (i * bs + idx)[:, None]
        lk = jnp.where(mask, jax.nn.log_sigmoid(-z), 0.0).reshape(B, H, bs, i + 1, bs)
        blk = lk.sum(-1)
        later = lax.cumsum(blk, axis=3, reverse=True) - blk
        incl = jnp.einsum('bhqnj,js->bhqns', lk, incl_mat) + later[..., None]
        w = jnp.where(mask, jnp.exp(z + incl.reshape(B, H, bs, n_k)), 0.0)
        outs.append(jnp.einsum('bhqk,bkhd->bqhd', w.astype(v.dtype), vi))
    return jnp.concatenate(outs, axis=1).reshape(B, L, H * Dh)


def stick_breaking_step(q, k_all, v_all, q_pos0, k_pos0):
    f32 = jnp.float32
    B, L, H, Dh = q.shape
    Nk = k_all.shape[1]
    kpos = k_pos0 + jnp.arange(Nk, dtype=jnp.int32)
    qpos = q_pos0 + jnp.arange(L, dtype=jnp.int32)
    z = jnp.einsum('bqhd,bkhd->bhqk', q, k_all).astype(f32) * (Dh ** -0.5)
    mask = kpos[None, :] < qpos[:, None]
    lk = jnp.where(mask, jax.nn.log_sigmoid(-z), 0.0)
    incl = lax.cumsum(lk, axis=3, reverse=True)
    w = jnp.where(mask, jnp.exp(z + incl), 0.0)
    out = jnp.einsum('bhqk,bkhd->bqhd', w.astype(v_all.dtype), v_all)
    return out.reshape(B, L, H * Dh)


def trunk_layer(x, c, pool_hist, s5_re, s5_im, k_cache, v_cache, pos0,
                w_ada, b_ada, w_in, w_pool, pool_scale, s5_a_re, s5_a_im, s5_log_dt,
                s5_b_re, s5_b_im, s5_c_re, s5_c_im, s5_d, w_glu, w_out, ln1_g, ln1_b,
                w_up, w_down, ln2_g, ln2_b):
    B, L, _ = x.shape
    ada = jax.nn.silu(c) @ w_ada + b_ada
    sh1, sc1, g1, sh2, sc2, g2 = jnp.split(ada[:, None, :], 6, axis=-1)
    h = layer_norm(x) * (1.0 + sc1) + sh1
    proj = h @ w_in
    o1 = POOL_WIDTH
    o2 = o1 + S5_WIDTH
    o3 = o2 + SB_WIDTH
    o4 = o3 + SB_WIDTH
    p, u, q, k, v = jnp.split(proj, [o1, o2, o3, o4], axis=-1)
    q = q.reshape(B, L, SB_HEADS, SB_HEAD_DIM)
    k = k.reshape(B, L, SB_HEADS, SB_HEAD_DIM)
    v = v.reshape(B, L, SB_HEADS, SB_HEAD_DIM)
    pool_out, new_hist = pool_mixer(p, pool_hist, pos0, w_pool, pool_scale)
    s5_out, new_re, new_im = s5_mixer(u, s5_re, s5_im, s5_a_re, s5_a_im, s5_log_dt,
                                      s5_b_re, s5_b_im, s5_c_re, s5_c_im, s5_d, w_glu)
    if k_cache is None:
        attn = stick_breaking_prompt(q, k, v)
    else:
        k_all = jnp.concatenate([k_cache.astype(k.dtype), k], axis=1)
        v_all = jnp.concatenate([v_cache.astype(v.dtype), v], axis=1)
        attn = stick_breaking_step(q, k_all, v_all, pos0, pos0 - k_cache.shape[1])
    mix = jnp.concatenate([pool_out, s5_out, attn], axis=-1) @ w_out
    x = layer_norm(DEEPNORM_ALPHA * x + g1 * mix, ln1_g, ln1_b)
    h2 = layer_norm(x) * (1.0 + sc2) + sh2
    ff = jnp.square(jax.nn.relu(h2 @ w_up)) @ w_down
    x = layer_norm(DEEPNORM_ALPHA * x + g2 * ff, ln2_g, ln2_b)
    return x, new_hist, new_re, new_im, k, v


def run_group(x, c, pool_hist, s5_re, s5_im, k_cache, v_cache, pos0, weights):
    new = ([], [], [], [], [])
    for l in range(DEPTH):
        kc = None if k_cache is None else k_cache[l]
        vc = None if v_cache is None else v_cache[l]
        x, *st = trunk_layer(x, c, pool_hist[l], s5_re[l], s5_im[l], kc, vc,
                             pos0, *[w[l] for w in weights])
        for lst, s in zip(new, st):
            lst.append(s)
    return x, [jnp.stack(s) for s in new]


def setup_inputs(seed: int = 0) -> dict:
    key = jax.random.key(seed)
    ks = jax.random.split(key, 40)
    f32 = jnp.float32

    def nrm(k, shape, scale=1.0):
        return jax.random.normal(k, shape, f32) * scale

    d_inv = D_MODEL ** -0.5
    n_idx = jnp.arange(S5_STATE, dtype=f32)
    return {
        'x_prompt': nrm(ks[0], (BATCH, SEQ, D_MODEL)),
        'x_sample': nrm(ks[1], (DEC_BATCH, DEC_SEQ, D_MODEL)),
        'state_pool': nrm(ks[2], (DEPTH, DEC_BATCH, POOL_HIST, POOL_WIDTH)),
        'state_s5_re': nrm(ks[3], (DEPTH, DEC_BATCH, S5_GROUPS, S5_STATE), 0.1),
        'state_s5_im': nrm(ks[4], (DEPTH, DEC_BATCH, S5_GROUPS, S5_STATE), 0.1),
        'cache_k': nrm(ks[5], (DEPTH, DEC_BATCH, PAST_LEN, SB_HEADS, SB_HEAD_DIM)),
        'cache_v': nrm(ks[6], (DEPTH, DEC_BATCH, PAST_LEN, SB_HEADS, SB_HEAD_DIM)),
        'c_prompt': nrm(ks[7], (BATCH, D_MODEL)),
        'c_sample': nrm(ks[8], (DEC_BATCH, D_MODEL)),
        'w_ada': nrm(ks[9], (DEPTH, D_MODEL, 6 * D_MODEL), d_inv),
        'b_ada': nrm(ks[10], (DEPTH, 6 * D_MODEL), 0.01),
        'w_in': nrm(ks[11], (DEPTH, D_MODEL, IN_WIDTH), d_inv),
        'w_pool': nrm(ks[12], (DEPTH, len(POOL_WINDOWS), POOL_GROUP, POOL_GROUP), POOL_GROUP ** -0.5),
        'pool_scale': 1.0 + nrm(ks[13], (DEPTH, POOL_WIDTH), 0.1),
        's5_a_re': -0.5 + nrm(ks[14], (DEPTH, S5_GROUPS, S5_STATE), 0.01),
        's5_a_im': math.pi * n_idx + nrm(ks[15], (DEPTH, S5_GROUPS, S5_STATE), 0.01),
        's5_log_dt': jax.random.uniform(ks[16], (DEPTH, S5_GROUPS), f32, math.log(1e-3), math.log(1e-1)),
        's5_b_re': nrm(ks[17], (DEPTH, S5_GROUPS, S5_STATE, S5_GROUP_CH), (2 * S5_GROUP_CH) ** -0.5),
        's5_b_im': nrm(ks[18], (DEPTH, S5_GROUPS, S5_STATE, S5_GROUP_CH), (2 * S5_GROUP_CH) ** -0.5),
        's5_c_re': nrm(ks[19], (DEPTH, S5_GROUPS, S5_GROUP_CH, S5_STATE), S5_STATE ** -0.5),
        's5_c_im': nrm(ks[20], (DEPTH, S5_GROUPS, S5_GROUP_CH, S5_STATE), S5_STATE ** -0.5),
        's5_d': nrm(ks[21], (DEPTH, S5_GROUPS, S5_GROUP_CH)),
        'w_glu': nrm(ks[22], (DEPTH, S5_WIDTH, S5_WIDTH), S5_WIDTH ** -0.5),
        'w_out': nrm(ks[23], (DEPTH, MIX_WIDTH, D_MODEL), MIX_WIDTH ** -0.5 * DEEPNORM_BETA),
        'ln1_g': 1.0 + nrm(ks[24], (DEPTH, D_MODEL), 0.02),
        'ln1_b': nrm(ks[25], (DEPTH, D_MODEL), 0.02),
        'w_up': nrm(ks[26], (DEPTH, D_MODEL, D_FF), d_inv),
        'w_down': nrm(ks[27], (DEPTH, D_FF, D_MODEL), D_FF ** -0.5 * DEEPNORM_BETA),
        'ln2_g': 1.0 + nrm(ks[28], (DEPTH, D_MODEL), 0.02),
        'ln2_b': nrm(ks[29], (DEPTH, D_MODEL), 0.02),
    }


def reference(x_prompt, x_sample, state_pool, state_s5_re, state_s5_im, cache_k, cache_v,
              c_prompt, c_sample, w_ada, b_ada, w_in, w_pool, pool_scale, s5_a_re, s5_a_im,
              s5_log_dt, s5_b_re, s5_b_im, s5_c_re, s5_c_im, s5_d, w_glu, w_out, ln1_g, ln1_b,
              w_up, w_down, ln2_g, ln2_b):
    weights = (w_ada, b_ada, w_in, w_pool, pool_scale, s5_a_re, s5_a_im, s5_log_dt,
               s5_b_re, s5_b_im, s5_c_re, s5_c_im, s5_d, w_glu, w_out, ln1_g, ln1_b,
               w_up, w_down, ln2_g, ln2_b)
    B = x_prompt.shape[0]
    p_hist0 = jnp.zeros((DEPTH, B, POOL_HIST, POOL_WIDTH), x_prompt.dtype)
    p_s5_0 = jnp.zeros((DEPTH, B, S5_GROUPS, S5_STATE), state_s5_re.dtype)
    y_prompt, (pool_p, s5_re_p, s5_im_p, k_p, v_p) = run_group(
        x_prompt, c_prompt, p_hist0, p_s5_0, p_s5_0, None, None, 0, weights)
    y_sample, (pool_s, s5_re_s, s5_im_s, k_s, v_s) = run_group(
        x_sample, c_sample, state_pool, state_s5_re, state_s5_im, cache_k, cache_v,
        cache_k.shape[2], weights)
    return (y_prompt, y_sample, pool_p, s5_re_p, s5_im_p, k_p, v_p,
            pool_s, s5_re_s, s5_im_s, k_s, v_s)
```

```python
import functools
import math

import jax
import jax.numpy as jnp
from jax import lax
from jax.experimental import pallas as pl
from jax.experimental.pallas import tpu as pltpu

F32 = jnp.float32
BF16 = jnp.bfloat16

D_MODEL = 2048
DEPTH = 2
POOL_WIDTH = 512
POOL_WINDOWS = (2, 4, 8, 16)
POOL_GROUP = 128
POOL_HIST = 15
HIST_ROWS = 16
S5_WIDTH = 512
S5_GROUP_CH = 16
S5_GROUPS = 32
S5_STATE = 64
S5_FLAT = S5_GROUPS * S5_STATE
SB_WIDTH = 1024
SB_HEAD_DIM = 128
SB_HEADS = 8
D_FF = 4 * D_MODEL
IN_WIDTH = 4096
COL_TILE = 1024
DEEPNORM_ALPHA = (2 * DEPTH) ** 0.25
LN_EPS = 1e-5

V7X_VMEM_LIMIT = 56 * 1024 * 1024
SUBLANES = 8
ROW_TILE = 512
S5_CHUNK = 256
S5_LANE_CHUNK = 512
ATT_BLOCK = 128
STICK_LOG_FLOOR = -104.0


def _params(*sem):
    return pltpu.CompilerParams(dimension_semantics=sem, vmem_limit_bytes=V7X_VMEM_LIMIT)


def _layer_norm(x):
    mu = jnp.mean(x, axis=-1, keepdims=True)
    xc = x - mu
    var = jnp.mean(xc * xc, axis=-1, keepdims=True)
    return xc * lax.rsqrt(var + LN_EPS)


def _ada_kernel(c_ref, w_ref, b_ref, o_ref):
    c = c_ref[...]
    s = (c * jax.nn.sigmoid(c)).astype(BF16)
    w = w_ref[0].astype(BF16)
    o_ref[0] = jnp.dot(s, w, preferred_element_type=F32) + b_ref[0]


def _ada(c_all, w_ada, b_ada):
    rows = c_all.shape[0]
    n_out = w_ada.shape[-1]
    tn = 1024
    return pl.pallas_call(
        _ada_kernel,
        grid=(DEPTH, n_out // tn),
        in_specs=[
            pl.BlockSpec((rows, D_MODEL), lambda l, j: (0, 0)),
            pl.BlockSpec((1, D_MODEL, tn), lambda l, j: (l, 0, j)),
            pl.BlockSpec((1, 1, tn), lambda l, j: (l, 0, j)),
        ],
        out_specs=pl.BlockSpec((1, rows, tn), lambda l, j: (l, 0, j)),
        out_shape=jax.ShapeDtypeStruct((DEPTH, rows, n_out), F32),
        compiler_params=_params("arbitrary", "arbitrary"),
        name="ada",
    )(c_all, w_ada, b_ada.reshape(DEPTH, 1, n_out))


def _s5prep_kernel(are_ref, aim_ref, ldt_ref, bre_ref, bim_ref, tab_ref, bb_ref):
    a_re = are_ref[0]
    a_im = aim_ref[0]
    dt = jnp.exp(ldt_ref[0])
    row = lax.broadcasted_iota(jnp.int32, (SUBLANES, S5_FLAT), 0)
    kf = (row + 1).astype(F32)
    mag = jnp.exp(kf * (a_re * dt))
    ang = kf * (a_im * dt)
    p_re = mag * jnp.cos(ang)
    p_im = mag * jnp.sin(ang)
    tab_ref[0, 0] = p_re
    tab_ref[0, 1] = p_im
    for n, k in enumerate((1, 2, 4)):
        keep = row >= k
        tab_ref[0, 2 + 2 * n] = jnp.where(keep, p_re[k - 1:k, :], 0.0)
        tab_ref[0, 3 + 2 * n] = jnp.where(keep, p_im[k - 1:k, :], 0.0)
    ab_re = p_re[0:1, :]
    ab_im = p_im[0:1, :]
    den = a_re * a_re + a_im * a_im
    f_re = ((ab_re - 1.0) * a_re + ab_im * a_im) / den
    f_im = (ab_im * a_re - (ab_re - 1.0) * a_im) / den
    b_re = bre_ref[0]
    b_im = bim_ref[0]
    bb_ref[0, 0] = f_re * b_re - f_im * b_im
    bb_ref[0, 1] = f_re * b_im + f_im * b_re


def _s5prep(a_re, a_im, log_dt, b_re, b_im):
    flat = lambda a: a.reshape(DEPTH, 1, S5_FLAT)
    ldt = jnp.broadcast_to(log_dt[:, :, None], (DEPTH, S5_GROUPS, S5_STATE)).reshape(DEPTH, 1, S5_FLAT)
    to_ch_major = lambda b: jnp.transpose(b, (0, 3, 1, 2)).reshape(DEPTH, S5_GROUP_CH, S5_FLAT)
    row_spec = pl.BlockSpec((1, 1, S5_FLAT), lambda l: (l, 0, 0))
    b_spec = pl.BlockSpec((1, S5_GROUP_CH, S5_FLAT), lambda l: (l, 0, 0))
    return pl.pallas_call(
        _s5prep_kernel,
        grid=(DEPTH,),
        in_specs=[row_spec, row_spec, row_spec, b_spec, b_spec],
        out_specs=[
            pl.BlockSpec((1, 8, SUBLANES, S5_FLAT), lambda l: (l, 0, 0, 0)),
            pl.BlockSpec((1, 2, S5_GROUP_CH, S5_FLAT), lambda l: (l, 0, 0, 0)),
        ],
        out_shape=[
            jax.ShapeDtypeStruct((DEPTH, 8, SUBLANES, S5_FLAT), F32),
            jax.ShapeDtypeStruct((DEPTH, 2, S5_GROUP_CH, S5_FLAT), F32),
        ],
        compiler_params=_params("arbitrary"),
        name="s5prep",
    )(flat(a_re), flat(a_im), ldt, to_ch_major(b_re), to_ch_major(b_im))


def _s5_dense_weights(bb, c_re, c_im):
    ch_group = jnp.arange(S5_WIDTH)[:, None] // S5_GROUP_CH
    st_group = jnp.arange(S5_FLAT)[None, :] // S5_STATE
    same = ch_group == st_group
    b_dense = [jnp.where(same, jnp.tile(bb[part], (S5_GROUPS, 1)), 0.0) for part in range(2)]
    b_cat = jnp.concatenate(b_dense, axis=1).astype(BF16)

    def c_dense(c):
        c_sp = jnp.transpose(c, (0, 2, 1)).reshape(S5_FLAT, S5_GROUP_CH)
        return jnp.where(same.T, jnp.tile(c_sp, (1, S5_GROUPS)), 0.0).astype(BF16)

    c_cat = jnp.concatenate([c_dense(c_re), c_dense(c_im)], axis=0)
    return b_cat, c_cat


def _inproj_kernel(x_ref, sc_ref, sh_ref, w_ref, f_ref, h_ref, hs_ref):
    j = pl.program_id(2)
    bb, tl, _ = x_ref.shape
    rows = bb * tl

    @pl.when(j == 0)
    def _():
        h = _layer_norm(x_ref[...]) * (1.0 + sc_ref[...]) + sh_ref[...]
        hs_ref[...] = h.reshape(rows, D_MODEL).astype(BF16)

    r = jnp.dot(hs_ref[...], w_ref[...], preferred_element_type=F32).reshape(bb, tl, COL_TILE)

    @pl.when(j != 1)
    def _():
        f_ref[0] = r

    @pl.when(j >= 1)
    def _():
        h_ref[0] = r.astype(BF16)


def _inproj(x, sc, sh, w_in, bb, tl):
    B, L, _ = x.shape
    n_col = IN_WIDTH // COL_TILE
    out_idx = lambda b, i, j: (jnp.maximum(j - 1, 0), b, i, 0)
    return pl.pallas_call(
        _inproj_kernel,
        grid=(B // bb, L // tl, n_col),
        in_specs=[
            pl.BlockSpec((bb, tl, D_MODEL), lambda b, i, j: (b, i, 0)),
            pl.BlockSpec((bb, 1, D_MODEL), lambda b, i, j: (b, 0, 0)),
            pl.BlockSpec((bb, 1, D_MODEL), lambda b, i, j: (b, 0, 0)),
            pl.BlockSpec((D_MODEL, COL_TILE), lambda b, i, j: (0, j)),
        ],
        out_specs=[
            pl.BlockSpec((1, bb, tl, COL_TILE), out_idx),
            pl.BlockSpec((1, bb, tl, COL_TILE), out_idx),
        ],
        out_shape=[
            jax.ShapeDtypeStruct((3, B, L, COL_TILE), F32),
            jax.ShapeDtypeStruct((3, B, L, COL_TILE), BF16),
        ],
        scratch_shapes=[pltpu.VMEM((bb * tl, D_MODEL), BF16)],
        compiler_params=_params("arbitrary", "arbitrary", "arbitrary"),
        name="inproj",
    )(x, sc, sh, w_in)


def _gelu_tanh(y):
    return 0.5 * y * (1.0 + jnp.tanh(math.sqrt(2.0 / math.pi) * (y + 0.044715 * (y * y * y))))


def _pools5_kernel(pos0, pu_ref, hist_ref, h0_ref, wpool_ref, pscale_ref, tab_ref, bcat_ref,
                   ccat_ref, dskip_ref, wglu_ref, mix_ref, nhist_ref, nstate_ref,
                   ext_ref, bu_ref, carry_ref):
    i = pl.program_id(1)
    T = pu_ref.shape[2]

    @pl.when(i == 0)
    def _():
        ext_ref[0:HIST_ROWS, :] = hist_ref[0]
        carry_ref[...] = h0_ref[0]

    p = pu_ref[0, 0, :, 0:POOL_WIDTH]
    ext_ref[HIST_ROWS:HIST_ROWS + T, :] = p
    pos = pos0 + i * T + lax.broadcasted_iota(jnp.int32, (T, POOL_GROUP), 0)
    for g, w in enumerate(POOL_WINDOWS):
        lo, hi = g * POOL_GROUP, (g + 1) * POOL_GROUP
        win = ext_ref[HIST_ROWS:HIST_ROWS + T, lo:hi]
        for k in range(1, w):
            win = win + ext_ref[HIST_ROWS - k:HIST_ROWS - k + T, lo:hi]
        cnt = jnp.minimum(w, pos + 1).astype(F32)
        mixed = win / cnt - ext_ref[HIST_ROWS:HIST_ROWS + T, lo:hi]
        out = jnp.dot(mixed.astype(BF16), wpool_ref[g], preferred_element_type=F32)
        mix_ref[0, :, lo:hi] = (out * pscale_ref[:, lo:hi]).astype(BF16)
    last = ext_ref[T:T + HIST_ROWS, :]
    nhist_ref[0] = last
    ext_ref[0:HIST_ROWS, :] = last

    u = pu_ref[0, 0, :, POOL_WIDTH:POOL_WIDTH + S5_WIDTH]
    bu_ref[...] = jnp.dot(u.astype(BF16), bcat_ref[...], preferred_element_type=F32)

    lc = S5_LANE_CHUNK
    for c in range(S5_FLAT // lc):
        re_l = slice(c * lc, (c + 1) * lc)
        im_l = slice(S5_FLAT + c * lc, S5_FLAT + (c + 1) * lc)

        def tile_step(r, carry, re_l=re_l, im_l=im_l):
            c_re, c_im = carry
            rows = pl.ds(pl.multiple_of(r * SUBLANES, SUBLANES), SUBLANES)
            b_re = bu_ref[rows, re_l]
            b_im = bu_ref[rows, im_l]
            for n, k in enumerate((1, 2, 4)):
                s_re = pltpu.roll(b_re, k, 0)
                s_im = pltpu.roll(b_im, k, 0)
                m_re = tab_ref[0, 2 + 2 * n, :, re_l]
                m_im = tab_ref[0, 3 + 2 * n, :, re_l]
                b_re, b_im = (b_re + (m_re * s_re - m_im * s_im),
                              b_im + (m_re * s_im + m_im * s_re))
            p_re = tab_ref[0, 0, :, re_l]
            p_im = tab_ref[0, 1, :, re_l]
            h_re = b_re + (p_re * c_re - p_im * c_im)
            h_im = b_im + (p_re * c_im + p_im * c_re)
            bu_ref[rows, re_l] = h_re
            bu_ref[rows, im_l] = h_im
            return (jnp.broadcast_to(h_re[SUBLANES - 1:SUBLANES, :], (SUBLANES, lc)),
                    jnp.broadcast_to(h_im[SUBLANES - 1:SUBLANES, :], (SUBLANES, lc)))

        c_re, c_im = lax.fori_loop(0, T // SUBLANES, tile_step,
                                   (carry_ref[:, re_l], carry_ref[:, im_l]))
        carry_ref[:, re_l] = c_re
        carry_ref[:, im_l] = c_im

    nstate_ref[0] = carry_ref[0:1, :]
    y = (jnp.dot(bu_ref[:, 0:S5_FLAT].astype(BF16), ccat_ref[0:S5_FLAT, :], preferred_element_type=F32)
         - jnp.dot(bu_ref[:, S5_FLAT:2 * S5_FLAT].astype(BF16), ccat_ref[S5_FLAT:2 * S5_FLAT, :],
                   preferred_element_type=F32)
         + dskip_ref[...] * u)
    y = _gelu_tanh(y)
    gate = jnp.dot(y.astype(BF16), wglu_ref[...], preferred_element_type=F32)
    mix_ref[0, :, POOL_WIDTH:POOL_WIDTH + S5_WIDTH] = (y * jax.nn.sigmoid(gate)).astype(BF16)


def _pools5(proj_f32, hist, h0, w_pool, pool_scale, tab, b_cat, c_cat, d_skip, w_glu, pos0, chunk):
    _, B, L, _ = proj_f32.shape
    T = chunk
    const2 = lambda b, i: (0, 0)
    return pl.pallas_call(
        functools.partial(_pools5_kernel, pos0),
        grid=(B, L // T),
        in_specs=[
            pl.BlockSpec((1, 1, T, COL_TILE), lambda b, i: (0, b, i, 0)),
            pl.BlockSpec((1, HIST_ROWS, POOL_WIDTH), lambda b, i: (b, 0, 0)),
            pl.BlockSpec((1, SUBLANES, 2 * S5_FLAT), lambda b, i: (b, 0, 0)),
            pl.BlockSpec((len(POOL_WINDOWS), POOL_GROUP, POOL_GROUP), lambda b, i: (0, 0, 0)),
            pl.BlockSpec((1, POOL_WIDTH), const2),
            pl.BlockSpec((1, 8, SUBLANES, S5_FLAT), lambda b, i: (0, 0, 0, 0)),
            pl.BlockSpec((S5_WIDTH, 2 * S5_FLAT), const2),
            pl.BlockSpec((2 * S5_FLAT, S5_WIDTH), const2),
            pl.BlockSpec((1, S5_WIDTH), const2),
            pl.BlockSpec((S5_WIDTH, S5_WIDTH), const2),
        ],
        out_specs=[
            pl.BlockSpec((1, T, COL_TILE), lambda b, i: (b, i, 0)),
            pl.BlockSpec((1, HIST_ROWS, POOL_WIDTH), lambda b, i: (b, 0, 0)),
            pl.BlockSpec((1, 1, 2 * S5_FLAT), lambda b, i: (b, 0, 0)),
        ],
        out_shape=[
            jax.ShapeDtypeStruct((B, L, COL_TILE), BF16),
            jax.ShapeDtypeStruct((B, HIST_ROWS, POOL_WIDTH), F32),
            jax.ShapeDtypeStruct((B, 1, 2 * S5_FLAT), F32),
        ],
        scratch_shapes=[
            pltpu.VMEM((HIST_ROWS + T, POOL_WIDTH), F32),
            pltpu.VMEM((T, 2 * S5_FLAT), F32),
            pltpu.VMEM((SUBLANES, 2 * S5_FLAT), F32),
        ],
        compiler_params=_params("arbitrary", "arbitrary"),
        name="pools5",
    )(proj_f32, hist, h0, w_pool, pool_scale, tab, b_cat, c_cat, d_skip, w_glu)


def _neg_softplus(z):
    return -(jnp.maximum(z, 0.0) + jnp.log1p(jnp.exp(-jnp.abs(z))))


def _tri_ones(n):
    j = lax.broadcasted_iota(jnp.int32, (n, 2 * n), 0)
    s = lax.broadcasted_iota(jnp.int32, (n, 2 * n), 1)
    return jnp.where((j >= s) | (s >= n), 1.0, 0.0).astype(BF16)


def _stick_block(q, k, v, mask, later, tri):
    n = k.shape[0]
    z = lax.dot_general(q, k, (((1,), (1,)), ((), ())), preferred_element_type=F32) * (SB_HEAD_DIM ** -0.5)
    lk = _neg_softplus(z)
    if mask is not None:
        lk = jnp.where(mask, lk, 0.0)
    lk_hi = lk.astype(BF16)
    lk_lo = (lk - lk_hi.astype(F32)).astype(BF16)
    s = (jnp.dot(lk_hi, tri, preferred_element_type=F32) + jnp.dot(lk_lo, tri, preferred_element_type=F32))
    incl = s[:, 0:n] + later
    w = jnp.exp(z + incl)
    if mask is not None:
        w = jnp.where(mask, w, 0.0)
    out = jnp.dot(w.astype(BF16), v, preferred_element_type=F32)
    return out, later + s[:, n:2 * n]


def _attn_prompt_kernel(q_ref, k_ref, v_ref, o_ref):
    i = pl.program_id(2)
    blk = q_ref.shape[2]
    q = q_ref[0, 0]
    tri = _tri_ones(blk)
    row = lax.broadcasted_iota(jnp.int32, (blk, blk), 0)
    col = lax.broadcasted_iota(jnp.int32, (blk, blk), 1)

    def cond(carry):
        j, go, _, _ = carry
        return jnp.logical_and(j >= 0, go > 0)

    def body(carry):
        j, _, acc, later = carry
        rows = pl.ds(pl.multiple_of(j * blk, blk), blk)
        mask = (j * blk + col) < (i * blk + row)
        out, later = _stick_block(q, k_ref[0, 0, rows, :], v_ref[0, 0, rows, :], mask, later, tri)
        go = (jnp.max(later) >= STICK_LOG_FLOOR).astype(jnp.int32)
        return j - 1, go, acc + out, later

    zeros = jnp.zeros((blk, blk), F32)
    _, _, acc, _ = lax.while_loop(cond, body, (i, jnp.int32(1), zeros, zeros))
    o_ref[0] = acc.astype(BF16)


def _attn_prompt(proj_bf):
    _, B, L, _ = proj_bf.shape
    blk = min(ATT_BLOCK, L)
    kv_spec = lambda part: pl.BlockSpec((1, 1, L, SB_HEAD_DIM), lambda b, h, i: (part, b, 0, h))
    return pl.pallas_call(
        _attn_prompt_kernel,
        grid=(B, SB_HEADS, L // blk),
        in_specs=[
            pl.BlockSpec((1, 1, blk, SB_HEAD_DIM), lambda b, h, i: (0, b, i, h)),
            kv_spec(1),
            kv_spec(2),
        ],
        out_specs=pl.BlockSpec((1, blk, SB_HEAD_DIM), lambda b, h, i: (b, i, h)),
        out_shape=jax.ShapeDtypeStruct((B, L, SB_WIDTH), BF16),
        compiler_params=_params("arbitrary", "arbitrary", "arbitrary"),
        name="attn_prompt",
    )(proj_bf, proj_bf, proj_bf)


def _attn_sample_kernel(q_ref, kn_ref, vn_ref, kc_ref, vc_ref, o_ref):
    L = q_ref.shape[2]
    past = kc_ref.shape[2]
    cblk = min(ATT_BLOCK, past)
    tri_new = _tri_ones(L)
    tri_c = _tri_ones(cblk)
    row = lax.broadcasted_iota(jnp.int32, (L, L), 0)
    col = lax.broadcasted_iota(jnp.int32, (L, L), 1)
    causal = col < row
    for h in range(SB_HEADS):
        lanes = slice(h * SB_HEAD_DIM, (h + 1) * SB_HEAD_DIM)
        q = q_ref[0, 0, :, lanes]
        acc, later = _stick_block(q, kn_ref[0, 0, :, lanes], vn_ref[0, 0, :, lanes], causal,
                                  jnp.zeros((L, L), F32), tri_new)
        later = jnp.broadcast_to(later[:, 0:1], (L, cblk))
        for j in range(past // cblk - 1, -1, -1):
            rows = slice(j * cblk, (j + 1) * cblk)
            out, later = _stick_block(q, kc_ref[0, 0, rows, lanes].astype(BF16),
                                      vc_ref[0, 0, rows, lanes].astype(BF16), None, later, tri_c)
            acc = acc + out
        o_ref[0, :, lanes] = acc.astype(BF16)


def _attn_sample(proj_bf, cache_k, cache_v, layer):
    _, B, L, _ = proj_bf.shape
    past = cache_k.shape[2]
    new_spec = lambda part: pl.BlockSpec((1, 1, L, SB_WIDTH), lambda b: (part, b, 0, 0))
    cache_spec = pl.BlockSpec((1, 1, past, SB_WIDTH), lambda b: (layer, b, 0, 0))
    return pl.pallas_call(
        _attn_sample_kernel,
        grid=(B,),
        in_specs=[new_spec(0), new_spec(1), new_spec(2), cache_spec, cache_spec],
        out_specs=pl.BlockSpec((1, L, SB_WIDTH), lambda b: (b, 0, 0)),
        out_shape=jax.ShapeDtypeStruct((B, L, SB_WIDTH), BF16),
        compiler_params=_params("arbitrary"),
        name="attn_sample",
    )(proj_bf, proj_bf, proj_bf, cache_k, cache_v)


def _outproj_kernel(mix_ref, att_ref, x_ref, g_ref, w_ref, lng_ref, lnb_ref, o_ref):
    bb, tl, _ = x_ref.shape
    rows = bb * tl
    half = mix_ref.shape[-1]
    mix = (jnp.dot(mix_ref[...].reshape(rows, half), w_ref[0:half, :], preferred_element_type=F32)
           + jnp.dot(att_ref[...].reshape(rows, half), w_ref[half:2 * half, :], preferred_element_type=F32))
    t = DEEPNORM_ALPHA * x_ref[...] + g_ref[...] * mix.reshape(bb, tl, D_MODEL)
    o_ref[...] = _layer_norm(t) * lng_ref[...] + lnb_ref[...]


def _outproj(mix, att, x, gate, w_out, ln_g, ln_b, bb, tl):
    B, L, _ = x.shape
    tok = lambda width: pl.BlockSpec((bb, tl, width), lambda b, i: (b, i, 0))
    vec = pl.BlockSpec((1, 1, D_MODEL), lambda b, i: (0, 0, 0))
    return pl.pallas_call(
        _outproj_kernel,
        grid=(B // bb, L // tl),
        in_specs=[
            tok(COL_TILE), tok(SB_WIDTH), tok(D_MODEL),
            pl.BlockSpec((bb, 1, D_MODEL), lambda b, i: (b, 0, 0)),
            pl.BlockSpec((2 * COL_TILE, D_MODEL), lambda b, i: (0, 0)),
            vec, vec,
        ],
        out_specs=tok(D_MODEL),
        out_shape=jax.ShapeDtypeStruct((B, L, D_MODEL), F32),
        compiler_params=_params("arbitrary", "arbitrary"),
        name="outproj",
    )(mix, att, x, gate, w_out, ln_g.reshape(1, 1, D_MODEL), ln_b.reshape(1, 1, D_MODEL))


def _ffn_kernel(x_ref, sc_ref, sh_ref, g_ref, wup_ref, wdn_ref, lng_ref, lnb_ref, o_ref, hs_ref, acc_ref):
    f = pl.program_id(2)
    bb, tl, _ = x_ref.shape
    rows = bb * tl

    @pl.when(f == 0)
    def _():
        h = _layer_norm(x_ref[...]) * (1.0 + sc_ref[...]) + sh_ref[...]
        hs_ref[...] = h.reshape(rows, D_MODEL).astype(BF16)
        acc_ref[...] = jnp.zeros_like(acc_ref)

    hid = jnp.dot(hs_ref[...], wup_ref[...], preferred_element_type=F32)
    hid = jnp.square(jnp.maximum(hid, 0.0))
    acc_ref[...] += jnp.dot(hid.astype(BF16), wdn_ref[...], preferred_element_type=F32)

    @pl.when(f == pl.num_programs(2) - 1)
    def _():
        t = DEEPNORM_ALPHA * x_ref[...] + g_ref[...] * acc_ref[...].reshape(bb, tl, D_MODEL)
        o_ref[...] = _layer_norm(t) * lng_ref[...] + lnb_ref[...]


def _ffn(x, sc, sh, gate, w_up, w_down, ln_g, ln_b, bb, tl):
    B, L, _ = x.shape
    tf = 1024
    tok = pl.BlockSpec((bb, tl, D_MODEL), lambda b, i, f: (b, i, 0))
    mod = pl.BlockSpec((bb, 1, D_MODEL), lambda b, i, f: (b, 0, 0))
    vec = pl.BlockSpec((1, 1, D_MODEL), lambda b, i, f: (0, 0, 0))
    return pl.pallas_call(
        _ffn_kernel,
        grid=(B // bb, L // tl, D_FF // tf),
        in_specs=[
            tok, mod, mod, mod,
            pl.BlockSpec((D_MODEL, tf), lambda b, i, f: (0, f)),
            pl.BlockSpec((tf, D_MODEL), lambda b, i, f: (f, 0)),
            vec, vec,
        ],
        out_specs=tok,
        out_shape=jax.ShapeDtypeStruct((B, L, D_MODEL), F32),
        scratch_shapes=[pltpu.VMEM((bb * tl, D_MODEL), BF16), pltpu.VMEM((bb * tl, D_MODEL), F32)],
        compiler_params=_params("arbitrary", "arbitrary", "arbitrary"),
        name="ffn",
    )(x, sc, sh, gate, w_up, w_down, ln_g.reshape(1, 1, D_MODEL), ln_b.reshape(1, 1, D_MODEL))


def _row_blocking(B, L):
    tl = min(L, ROW_TILE)
    bb = max(1, min(B, ROW_TILE // tl))
    return bb, tl


def _run_group(x, ada, hist, s5_re, s5_im, cache_k, cache_v, pos0, lw):
    B, L, _ = x.shape
    bb, tl = _row_blocking(B, L)
    chunk = min(L, S5_CHUNK)
    new_hist, new_re, new_im, new_k, new_v = [], [], [], [], []
    for l in range(DEPTH):
        w = lw[l]
        sh1, sc1, g1, sh2, sc2, g2 = [ada[l][:, None, n * D_MODEL:(n + 1) * D_MODEL] for n in range(6)]
        proj_f32, proj_bf = _inproj(x, sc1, sh1, w["w_in"], bb, tl)
        hist16 = jnp.pad(hist[l], ((0, 0), (HIST_ROWS - POOL_HIST, 0), (0, 0)))
        h0 = jnp.concatenate([s5_re[l].reshape(B, 1, S5_FLAT), s5_im[l].reshape(B, 1, S5_FLAT)], axis=-1)
        h0 = jnp.broadcast_to(h0, (B, SUBLANES, 2 * S5_FLAT))
        mix, nh, ns = _pools5(proj_f32, hist16, h0, w["w_pool"], w["pool_scale"], w["tab"], w["b_cat"],
                              w["c_cat"], w["d_skip"], w["w_glu"], pos0, chunk)
        if cache_k is None:
            att = _attn_prompt(proj_bf)
        else:
            att = _attn_sample(proj_bf, cache_k, cache_v, l)
        x = _outproj(mix, att, x, g1, w["w_out"], w["ln1_g"], w["ln1_b"], bb, tl)
        x = _ffn(x, sc2, sh2, g2, w["w_up"], w["w_down"], w["ln2_g"], w["ln2_b"], bb, tl)
        new_hist.append(nh[:, HIST_ROWS - POOL_HIST:, :])
        new_re.append(ns[:, 0, 0:S5_FLAT].reshape(B, S5_GROUPS, S5_STATE))
        new_im.append(ns[:, 0, S5_FLAT:].reshape(B, S5_GROUPS, S5_STATE))
        new_k.append(proj_f32[1].reshape(B, L, SB_HEADS, SB_HEAD_DIM))
        new_v.append(proj_f32[2].reshape(B, L, SB_HEADS, SB_HEAD_DIM))
    stack = lambda xs: jnp.stack(xs)
    return x, (stack(new_hist), stack(new_re), stack(new_im), stack(new_k), stack(new_v))


def kernel(x_prompt, x_sample, state_pool, state_s5_re, state_s5_im, cache_k, cache_v, c_prompt, c_sample,
           w_ada, b_ada, w_in, w_pool, pool_scale, s5_a_re, s5_a_im, s5_log_dt, s5_b_re, s5_b_im,
           s5_c_re, s5_c_im, s5_d, w_glu, w_out, ln1_g, ln1_b, w_up, w_down, ln2_g, ln2_b):
    B = x_prompt.shape[0]
    Bs = x_sample.shape[0]
    past = cache_k.shape[2]

    n_c = B + Bs
    pad = (-n_c) % 16
    c_all = jnp.concatenate([c_prompt, c_sample, jnp.zeros((pad, D_MODEL), F32)], axis=0)
    ada = _ada(c_all, w_ada, b_ada)
    ada_p, ada_s = ada[:, 0:B], ada[:, B:n_c]

    tab, bb_disc = _s5prep(s5_a_re, s5_a_im, s5_log_dt, s5_b_re, s5_b_im)
    lw = []
    for l in range(DEPTH):
        b_cat, c_cat = _s5_dense_weights(bb_disc[l], s5_c_re[l], s5_c_im[l])
        lw.append(dict(
            w_in=w_in[l].astype(BF16), w_out=w_out[l].astype(BF16),
            w_up=w_up[l].astype(BF16), w_down=w_down[l].astype(BF16),
            w_glu=w_glu[l].astype(BF16), w_pool=w_pool[l].astype(BF16),
            pool_scale=pool_scale[l].reshape(1, POOL_WIDTH),
            d_skip=s5_d[l].reshape(1, S5_WIDTH),
            tab=tab[l:l + 1], b_cat=b_cat, c_cat=c_cat,
            ln1_g=ln1_g[l], ln1_b=ln1_b[l], ln2_g=ln2_g[l], ln2_b=ln2_b[l],
        ))

    zero_hist = jnp.zeros((DEPTH, B, POOL_HIST, POOL_WIDTH), F32)
    zero_s5 = jnp.zeros((DEPTH, B, S5_GROUPS, S5_STATE), F32)
    y_p, (pool_p, re_p, im_p, k_p, v_p) = _run_group(
        x_prompt, ada_p, zero_hist, zero_s5, zero_s5, None, None, 0, lw)
    ck = cache_k.reshape(DEPTH, Bs, past, SB_WIDTH)
    cv = cache_v.reshape(DEPTH, Bs, past, SB_WIDTH)
    y_s, (pool_s, re_s, im_s, k_s, v_s) = _run_group(
        x_sample, ada_s, state_pool, state_s5_re, state_s5_im, ck, cv, past, lw)
    return (y_p, y_s, pool_p, re_p, im_p, k_p, v_p, pool_s, re_s, im_s, k_s, v_s)
```

```python
import functools
import math

import jax
import jax.numpy as jnp
from jax import lax
from jax.experimental import pallas as pl
from jax.experimental.pallas import tpu as pltpu

F32 = jnp.float32
BF16 = jnp.bfloat16

D_MODEL = 2048
DEPTH = 2
POOL_WIDTH = 512
POOL_WINDOWS = (2, 4, 8, 16)
POOL_GROUP = 128
POOL_HIST = 15
HIST_ROWS = 16
S5_WIDTH = 512
S5_GROUP_CH = 16
S5_GROUPS = 32
S5_STATE = 64
S5_FLAT = S5_GROUPS * S5_STATE
SB_WIDTH = 1024
SB_HEAD_DIM = 128
SB_HEADS = 8
D_FF = 4 * D_MODEL
IN_WIDTH = 4096
COL_TILE = 1024
DEEPNORM_ALPHA = (2 * DEPTH) ** 0.25
LN_EPS = 1e-5

V7X_VMEM_LIMIT = 56 * 1024 * 1024
SUBLANES = 8
ROW_TILE = 512
ROW_CHUNKS = 4
FF_TILE = 1024
S5_CHUNK = 256
S5_LANE_CHUNK = 512
S5_BLOCKS = S5_FLAT // S5_LANE_CHUNK
S5_BLOCK_CH = S5_WIDTH // S5_BLOCKS
S5_TABLES = 12
SCAN_UNROLL = 4
ATT_BLOCK = 128
ATT_CHAINS = 16
LOG2_E = 1.4426950408889634
STICK_LOG2_FLOOR = -150.0
STICK_PARKED = -1e30


def _params(*sem):
    return pltpu.CompilerParams(dimension_semantics=sem, vmem_limit_bytes=V7X_VMEM_LIMIT)


def _layer_norm(x):
    mu = jnp.mean(x, axis=-1, keepdims=True)
    xc = x - mu
    var = jnp.mean(xc * xc, axis=-1, keepdims=True)
    return xc * lax.rsqrt(var + LN_EPS)


def _row_chunks(bb, tl):
    if bb >= ROW_CHUNKS:
        step = bb // ROW_CHUNKS
        return [(slice(c * step, (c + 1) * step), slice(0, tl)) for c in range(ROW_CHUNKS)]
    step = tl // ROW_CHUNKS
    return [(slice(0, bb), slice(c * step, (c + 1) * step)) for c in range(ROW_CHUNKS)]


def _cmul(a_re, a_im, b_re, b_im):
    return a_re * b_re - a_im * b_im, a_re * b_im + a_im * b_re


def _ada_kernel(c_ref, w_ref, b_ref, o_ref):
    c = c_ref[...]
    s = (c * jax.nn.sigmoid(c)).astype(BF16)
    w = w_ref[0].astype(BF16)
    o_ref[0] = jnp.dot(s, w, preferred_element_type=F32) + b_ref[0]


def _ada(c_all, w_ada, b_ada):
    rows = c_all.shape[0]
    n_out = w_ada.shape[-1]
    tn = 1024
    return pl.pallas_call(
        _ada_kernel,
        grid=(DEPTH, n_out // tn),
        in_specs=[
            pl.BlockSpec((rows, D_MODEL), lambda l, j: (0, 0)),
            pl.BlockSpec((1, D_MODEL, tn), lambda l, j: (l, 0, j)),
            pl.BlockSpec((1, 1, tn), lambda l, j: (l, 0, j)),
        ],
        out_specs=pl.BlockSpec((1, rows, tn), lambda l, j: (l, 0, j)),
        out_shape=jax.ShapeDtypeStruct((DEPTH, rows, n_out), F32),
        compiler_params=_params("arbitrary", "arbitrary"),
        name="ada",
    )(c_all, w_ada, b_ada.reshape(DEPTH, 1, n_out))


def _s5prep_kernel(seg, are_ref, aim_ref, ldt_ref, bre_ref, bim_ref, tab_ref, bb_ref):
    a_re = are_ref[0]
    a_im = aim_ref[0]
    dt = jnp.exp(ldt_ref[0])
    shape = (SUBLANES, S5_FLAT)
    row = lax.broadcasted_iota(jnp.int32, shape, 0)
    mag = jnp.broadcast_to(jnp.exp(a_re * dt), shape)
    ang = jnp.broadcast_to(a_im * dt, shape)
    ab_re = mag * jnp.cos(ang)
    ab_im = mag * jnp.sin(ang)
    tab_ref[0, 0] = ab_re
    tab_ref[0, 1] = ab_im
    p_re, p_im = ab_re, ab_im
    for _ in range(seg.bit_length() - 1):
        p_re, p_im = _cmul(p_re, p_im, p_re, p_im)
    ps_re = jnp.ones(shape, F32)
    ps_im = jnp.zeros(shape, F32)
    for n, k in enumerate((1, 2, 4)):
        tab_ref[0, 2 + 2 * n] = jnp.where(row >= k, p_re, 0.0)
        tab_ref[0, 3 + 2 * n] = jnp.where(row >= k, p_im, 0.0)
        q_re, q_im = _cmul(ps_re, ps_im, p_re, p_im)
        has_bit = (row & k) != 0
        ps_re = jnp.where(has_bit, q_re, ps_re)
        ps_im = jnp.where(has_bit, q_im, ps_im)
        p_re, p_im = _cmul(p_re, p_im, p_re, p_im)
    tab_ref[0, 8] = ps_re
    tab_ref[0, 9] = ps_im
    tab_ref[0, 10] = p_re
    tab_ref[0, 11] = p_im
    a1_re = ab_re[0:1, :]
    a1_im = ab_im[0:1, :]
    den = a_re * a_re + a_im * a_im
    f_re = ((a1_re - 1.0) * a_re + a1_im * a_im) / den
    f_im = (a1_im * a_re - (a1_re - 1.0) * a_im) / den
    b_re = bre_ref[0]
    b_im = bim_ref[0]
    bb_ref[0, 0] = f_re * b_re - f_im * b_im
    bb_ref[0, 1] = f_re * b_im + f_im * b_re


def _s5prep(a_re, a_im, log_dt, b_re, b_im, seg):
    assert seg & (seg - 1) == 0
    flat = lambda a: a.reshape(DEPTH, 1, S5_FLAT)
    ldt = jnp.broadcast_to(log_dt[:, :, None], (DEPTH, S5_GROUPS, S5_STATE)).reshape(DEPTH, 1, S5_FLAT)
    to_ch_major = lambda b: jnp.transpose(b, (0, 3, 1, 2)).reshape(DEPTH, S5_GROUP_CH, S5_FLAT)
    row_spec = pl.BlockSpec((1, 1, S5_FLAT), lambda l: (l, 0, 0))
    b_spec = pl.BlockSpec((1, S5_GROUP_CH, S5_FLAT), lambda l: (l, 0, 0))
    return pl.pallas_call(
        functools.partial(_s5prep_kernel, seg),
        grid=(DEPTH,),
        in_specs=[row_spec, row_spec, row_spec, b_spec, b_spec],
        out_specs=[
            pl.BlockSpec((1, S5_TABLES, SUBLANES, S5_FLAT), lambda l: (l, 0, 0, 0)),
            pl.BlockSpec((1, 2, S5_GROUP_CH, S5_FLAT), lambda l: (l, 0, 0, 0)),
        ],
        out_shape=[
            jax.ShapeDtypeStruct((DEPTH, S5_TABLES, SUBLANES, S5_FLAT), F32),
            jax.ShapeDtypeStruct((DEPTH, 2, S5_GROUP_CH, S5_FLAT), F32),
        ],
        compiler_params=_params("arbitrary"),
        name="s5prep",
    )(flat(a_re), flat(a_im), ldt, to_ch_major(b_re), to_ch_major(b_im))


def _s5_block_weights(bb, c_re, c_im):
    ch_group = jnp.arange(S5_WIDTH)[:, None] // S5_GROUP_CH
    st_group = jnp.arange(S5_FLAT)[None, :] // S5_STATE
    same = ch_group == st_group
    ch_b, st_b = S5_BLOCK_CH, S5_LANE_CHUNK

    def b_blocks(b):
        dense = jnp.where(same, jnp.tile(b, (S5_GROUPS, 1)), 0.0)
        return jnp.stack([dense[m * ch_b:(m + 1) * ch_b, m * st_b:(m + 1) * st_b] for m in range(S5_BLOCKS)])

    def c_blocks(c):
        c_sp = jnp.transpose(c, (0, 2, 1)).reshape(S5_FLAT, S5_GROUP_CH)
        dense = jnp.where(same.T, jnp.tile(c_sp, (1, S5_GROUPS)), 0.0)
        return jnp.stack([dense[m * st_b:(m + 1) * st_b, m * ch_b:(m + 1) * ch_b] for m in range(S5_BLOCKS)])

    b_blk = jnp.stack([b_blocks(bb[0]), b_blocks(bb[1])]).astype(BF16)
    c_blk = jnp.stack([c_blocks(c_re), c_blocks(c_im)]).astype(BF16)
    return b_blk, c_blk


def _inproj_kernel(x_ref, sc_ref, sh_ref, w_ref, *rest):
    pu_ref, k_ref, v_ref, bf_ref = rest[-4:]
    bb, tl, _ = x_ref.shape
    for bs, ts in _row_chunks(bb, tl):
        x = x_ref[bs, ts, :]
        cb, ct, _ = x.shape
        h = (_layer_norm(x) * (1.0 + sc_ref[bs]) + sh_ref[bs]).reshape(cb * ct, D_MODEL).astype(BF16)

        def cols(j, h=h, cb=cb, ct=ct):
            r = jnp.dot(h, w_ref[:, j * COL_TILE:(j + 1) * COL_TILE], preferred_element_type=F32)
            return r.reshape(cb, ct, COL_TILE)

        pu_ref[bs, ts, :] = cols(0)
        bf_ref[0, bs, ts, :] = cols(1).astype(BF16)
        k = cols(2)
        k_ref[0, bs, ts, :] = k
        bf_ref[1, bs, ts, :] = k.astype(BF16)
        v = cols(3)
        v_ref[0, bs, ts, :] = v
        bf_ref[2, bs, ts, :] = v.astype(BF16)


def _inproj(x, sc, sh, w_in, kv_prev, layer, bb, tl):
    B, L, _ = x.shape
    tok = lambda b, i: (b, i, 0)
    in_specs = [
        pl.BlockSpec((bb, tl, D_MODEL), tok),
        pl.BlockSpec((bb, 1, D_MODEL), lambda b, i: (b, 0, 0)),
        pl.BlockSpec((bb, 1, D_MODEL), lambda b, i: (b, 0, 0)),
        pl.BlockSpec((D_MODEL, IN_WIDTH), lambda b, i: (0, 0), pipeline_mode=pl.Buffered(1)),
    ]
    args = [x, sc, sh, w_in]
    aliases = {}
    if kv_prev is not None:
        in_specs += [pl.BlockSpec(memory_space=pl.ANY)] * 2
        args += list(kv_prev)
        aliases = {4: 1, 5: 2}
    kv_spec = pl.BlockSpec((1, bb, tl, COL_TILE), lambda b, i: (layer, b, i, 0))
    kv_shape = jax.ShapeDtypeStruct((DEPTH, B, L, COL_TILE), F32)
    return pl.pallas_call(
        _inproj_kernel,
        grid=(B // bb, L // tl),
        in_specs=in_specs,
        out_specs=[
            pl.BlockSpec((bb, tl, COL_TILE), tok),
            kv_spec, kv_spec,
            pl.BlockSpec((3, bb, tl, COL_TILE), lambda b, i: (0, b, i, 0)),
        ],
        out_shape=[
            jax.ShapeDtypeStruct((B, L, COL_TILE), F32),
            kv_shape, kv_shape,
            jax.ShapeDtypeStruct((3, B, L, COL_TILE), BF16),
        ],
        input_output_aliases=aliases,
        compiler_params=_params("arbitrary", "arbitrary"),
        name="inproj",
    )(*args)


def _gelu_tanh(y):
    return 0.5 * y * (1.0 + jnp.tanh(math.sqrt(2.0 / math.pi) * (y + 0.044715 * (y * y * y))))


def _pools5_kernel(pos0, pu_ref, hist_ref, h0_ref, wpool_ref, pscale_ref, tab_ref, bblk_ref,
                   cblk_ref, dskip_ref, wglu_ref, mix_ref, nhist_ref, nstate_ref,
                   ext_ref, bu_ref, carry_ref):
    i = pl.program_id(1)
    T = pu_ref.shape[1]

    @pl.when(i == 0)
    def _():
        ext_ref[0:HIST_ROWS, :] = hist_ref[0]
        carry_ref[...] = h0_ref[0]

    p = pu_ref[0, :, 0:POOL_WIDTH]
    ext_ref[HIST_ROWS:HIST_ROWS + T, :] = p
    pos = pos0 + i * T + lax.broadcasted_iota(jnp.int32, (T, POOL_GROUP), 0)
    for g, w in enumerate(POOL_WINDOWS):
        lo, hi = g * POOL_GROUP, (g + 1) * POOL_GROUP
        win = ext_ref[HIST_ROWS:HIST_ROWS + T, lo:hi]
        for k in range(1, w):
            win = win + ext_ref[HIST_ROWS - k:HIST_ROWS - k + T, lo:hi]
        cnt = jnp.minimum(w, pos + 1).astype(F32)
        mixed = win / cnt - ext_ref[HIST_ROWS:HIST_ROWS + T, lo:hi]
        out = jnp.dot(mixed.astype(BF16), wpool_ref[g], preferred_element_type=F32)
        mix_ref[0, :, lo:hi] = (out * pscale_ref[:, lo:hi]).astype(BF16)
    last = ext_ref[T:T + HIST_ROWS, :]
    nhist_ref[0] = last
    ext_ref[0:HIST_ROWS, :] = last

    seg = T // SUBLANES
    shift = seg.bit_length() - 1
    n_idx = lax.broadcasted_iota(jnp.int32, (T, T), 0)
    t_idx = lax.broadcasted_iota(jnp.int32, (T, T), 1)
    perm = jnp.where(t_idx == (n_idx & (SUBLANES - 1)) * seg + (n_idx >> 3), 1.0, 0.0).astype(BF16)
    unperm = jnp.where(t_idx == (n_idx & (seg - 1)) * SUBLANES + (n_idx >> shift), 1.0, 0.0).astype(BF16)

    u = pu_ref[0, :, POOL_WIDTH:POOL_WIDTH + S5_WIDTH]
    u_hi = u.astype(BF16)
    u_lo = (u - u_hi.astype(F32)).astype(BF16)
    up_hi = jnp.dot(perm, u_hi, preferred_element_type=F32)
    u_perm = up_hi + jnp.dot(perm, u_lo, preferred_element_type=F32)
    ub = up_hi.astype(BF16)

    lc = S5_LANE_CHUNK
    row8 = lax.broadcasted_iota(jnp.int32, (SUBLANES, lc), 0)
    y_parts = []
    for m in range(S5_BLOCKS):
        re_l = slice(m * lc, (m + 1) * lc)
        im_l = slice(S5_FLAT + m * lc, S5_FLAT + (m + 1) * lc)
        ch_l = slice(m * S5_BLOCK_CH, (m + 1) * S5_BLOCK_CH)
        bu_ref[:, re_l] = jnp.dot(ub[:, ch_l], bblk_ref[0, m], preferred_element_type=F32)
        bu_ref[:, im_l] = jnp.dot(ub[:, ch_l], bblk_ref[1, m], preferred_element_type=F32)
        a_re = tab_ref[0, 0, :, re_l]
        a_im = tab_ref[0, 1, :, re_l]

        def tile_rows(i):
            return pl.ds(pl.multiple_of(i * SUBLANES, SUBLANES), SUBLANES)

        def local_step(i, h, re_l=re_l, im_l=im_l, a_re=a_re, a_im=a_im):
            rows = tile_rows(i)
            g_re, g_im = _cmul(a_re, a_im, h[0], h[1])
            h_re = g_re + bu_ref[rows, re_l]
            h_im = g_im + bu_ref[rows, im_l]
            bu_ref[rows, re_l] = h_re
            bu_ref[rows, im_l] = h_im
            return h_re, h_im

        zeros = jnp.zeros((SUBLANES, lc), F32)
        f_re, f_im = lax.fori_loop(0, seg, local_step, (zeros, zeros), unroll=SCAN_UNROLL)

        for n, k in enumerate((1, 2, 4)):
            d_re, d_im = _cmul(tab_ref[0, 2 + 2 * n, :, re_l], tab_ref[0, 3 + 2 * n, :, re_l],
                               pltpu.roll(f_re, k, 0), pltpu.roll(f_im, k, 0))
            f_re, f_im = f_re + d_re, f_im + d_im
        c_re = carry_ref[:, re_l]
        c_im = carry_ref[:, im_l]
        e_re, e_im = _cmul(tab_ref[0, 8, :, re_l], tab_ref[0, 9, :, re_l], c_re, c_im)
        e_re = e_re + jnp.where(row8 >= 1, pltpu.roll(f_re, 1, 0), 0.0)
        e_im = e_im + jnp.where(row8 >= 1, pltpu.roll(f_im, 1, 0), 0.0)
        n_re, n_im = _cmul(tab_ref[0, 10, :, re_l], tab_ref[0, 11, :, re_l], c_re, c_im)
        carry_ref[:, re_l] = n_re + jnp.broadcast_to(f_re[SUBLANES - 1:SUBLANES, :], (SUBLANES, lc))
        carry_ref[:, im_l] = n_im + jnp.broadcast_to(f_im[SUBLANES - 1:SUBLANES, :], (SUBLANES, lc))

        def fix_step(i, w, re_l=re_l, im_l=im_l, a_re=a_re, a_im=a_im):
            rows = tile_rows(i)
            bu_ref[rows, re_l] = bu_ref[rows, re_l] + w[0]
            bu_ref[rows, im_l] = bu_ref[rows, im_l] + w[1]
            return _cmul(a_re, a_im, w[0], w[1])

        lax.fori_loop(0, seg, fix_step, _cmul(a_re, a_im, e_re, e_im), unroll=SCAN_UNROLL)

        y_parts.append(
            jnp.dot(bu_ref[:, re_l].astype(BF16), cblk_ref[0, m], preferred_element_type=F32)
            - jnp.dot(bu_ref[:, im_l].astype(BF16), cblk_ref[1, m], preferred_element_type=F32))

    nstate_ref[0] = carry_ref[0:1, :]
    y = _gelu_tanh(jnp.concatenate(y_parts, axis=1) + dskip_ref[...] * u_perm)
    gate = jnp.dot(y.astype(BF16), wglu_ref[...], preferred_element_type=F32)
    s5_perm = (y * jax.nn.sigmoid(gate)).astype(BF16)
    mix_ref[0, :, POOL_WIDTH:POOL_WIDTH + S5_WIDTH] = jnp.dot(
        unperm, s5_perm, preferred_element_type=F32).astype(BF16)


def _pools5(pu, hist, h0, w_pool, pool_scale, tab, b_blk, c_blk, d_skip, w_glu, pos0, chunk):
    B, L, _ = pu.shape
    T = chunk
    const2 = lambda b, i: (0, 0)
    const4 = lambda b, i: (0, 0, 0, 0)
    return pl.pallas_call(
        functools.partial(_pools5_kernel, pos0),
        grid=(B, L // T),
        in_specs=[
            pl.BlockSpec((1, T, COL_TILE), lambda b, i: (b, i, 0)),
            pl.BlockSpec((1, HIST_ROWS, POOL_WIDTH), lambda b, i: (b, 0, 0)),
            pl.BlockSpec((1, SUBLANES, 2 * S5_FLAT), lambda b, i: (b, 0, 0)),
            pl.BlockSpec((len(POOL_WINDOWS), POOL_GROUP, POOL_GROUP), lambda b, i: (0, 0, 0)),
            pl.BlockSpec((1, POOL_WIDTH), const2),
            pl.BlockSpec((1, S5_TABLES, SUBLANES, S5_FLAT), const4),
            pl.BlockSpec((2, S5_BLOCKS, S5_BLOCK_CH, S5_LANE_CHUNK), const4),
            pl.BlockSpec((2, S5_BLOCKS, S5_LANE_CHUNK, S5_BLOCK_CH), const4),
            pl.BlockSpec((1, S5_WIDTH), const2),
            pl.BlockSpec((S5_WIDTH, S5_WIDTH), const2),
        ],
        out_specs=[
            pl.BlockSpec((1, T, COL_TILE), lambda b, i: (b, i, 0)),
            pl.BlockSpec((1, HIST_ROWS, POOL_WIDTH), lambda b, i: (b, 0, 0)),
            pl.BlockSpec((1, 1, 2 * S5_FLAT), lambda b, i: (b, 0, 0)),
        ],
        out_shape=[
            jax.ShapeDtypeStruct((B, L, COL_TILE), BF16),
            jax.ShapeDtypeStruct((B, HIST_ROWS, POOL_WIDTH), F32),
            jax.ShapeDtypeStruct((B, 1, 2 * S5_FLAT), F32),
        ],
        scratch_shapes=[
            pltpu.VMEM((HIST_ROWS + T, POOL_WIDTH), F32),
            pltpu.VMEM((T, 2 * S5_FLAT), F32),
            pltpu.VMEM((SUBLANES, 2 * S5_FLAT), F32),
        ],
        compiler_params=_params("arbitrary", "arbitrary"),
        name="pools5",
    )(pu, hist, h0, w_pool, pool_scale, tab, b_blk, c_blk, d_skip, w_glu)


def _tri_ones(n):
    j = lax.broadcasted_iota(jnp.int32, (2 * n, 2 * n), 0)
    j = jnp.where(j >= n, j - n, j)
    s = lax.broadcasted_iota(jnp.int32, (2 * n, 2 * n), 1)
    return jnp.where((j >= s) | (s >= n), 1.0, 0.0).astype(BF16)


def _stick_blocks(qs, ks, vs, mask, laters, tri):
    n = ks[0].shape[0]
    zs = [lax.dot_general(q, k, (((1,), (1,)), ((), ())), preferred_element_type=F32)
          * (SB_HEAD_DIM ** -0.5 * LOG2_E) for q, k in zip(qs, ks)]
    sums = []
    for z in zs:
        sp = jnp.maximum(z, 0.0) + jnp.log2(1.0 + jnp.exp2(-jnp.abs(z)))
        if mask is not None:
            sp = jnp.where(mask, sp, 0.0)
        sp_hi = sp.astype(BF16)
        sp_lo = (sp - sp_hi.astype(F32)).astype(BF16)
        sums.append(jnp.dot(jnp.concatenate([sp_hi, sp_lo], axis=1), tri, preferred_element_type=F32))
    results = []
    for z, s, v, later in zip(zs, sums, vs, laters):
        w = jnp.exp2(z + (later - s[:, 0:n]))
        if mask is not None:
            w = jnp.where(mask, w, 0.0)
        results.append((jnp.dot(w.astype(BF16), v, preferred_element_type=F32), later - s[:, n:2 * n]))
    return results


def _attn_prompt_kernel(q_ref, k_ref, v_ref, o_ref, acc_ref, later_ref):
    tile = pl.program_id(2)
    blk = ATT_BLOCK
    n_chain = q_ref.shape[2] // blk
    tri = _tri_ones(blk)
    row = lax.broadcasted_iota(jnp.int32, (blk, blk), 0)
    col = lax.broadcasted_iota(jnp.int32, (blk, blk), 1)
    causal = col < row
    zeros = jnp.zeros((blk, blk), F32)

    def key_rows(kb):
        return pl.ds(pl.multiple_of(kb * blk, blk), blk)

    def chain_q():
        return [q_ref[0, 0, c * blk:(c + 1) * blk, :] for c in range(n_chain)]

    rows = [key_rows(tile * n_chain + c) for c in range(n_chain)]
    results = _stick_blocks(chain_q(), [k_ref[0, 0, r, :] for r in rows], [v_ref[0, 0, r, :] for r in rows],
                            causal, [zeros] * n_chain, tri)
    alive = None
    for c, (out, later) in enumerate(results):
        acc_ref[c] = out
        later_ref[c] = later
        alive = later if alive is None else jnp.maximum(alive, later)

    def cond(carry):
        t, go = carry
        return jnp.logical_and(t <= tile * n_chain + (n_chain - 1), go > 0)

    def body(carry):
        t, _ = carry
        kbs = [tile * n_chain + c - t for c in range(n_chain)]
        laters = [jnp.where(kb < 0, STICK_PARKED, later_ref[c]) for c, kb in enumerate(kbs)]
        rows = [key_rows(jnp.maximum(kb, 0)) for kb in kbs]
        results = _stick_blocks(chain_q(), [k_ref[0, 0, r, :] for r in rows],
                                [v_ref[0, 0, r, :] for r in rows], None, laters, tri)
        alive = None
        for c, (out, later) in enumerate(results):
            acc_ref[c] += out
            later_ref[c] = later
            alive = later if alive is None else jnp.maximum(alive, later)
        return t + 1, (jnp.max(alive) >= STICK_LOG2_FLOOR).astype(jnp.int32)

    go = (jnp.max(alive) >= STICK_LOG2_FLOOR).astype(jnp.int32)
    lax.while_loop(cond, body, (jnp.int32(1), go))
    for c in range(n_chain):
        o_ref[0, c * blk:(c + 1) * blk, :] = acc_ref[c].astype(BF16)


def _attn_prompt(proj_bf):
    _, B, L, _ = proj_bf.shape
    tq = min(ATT_CHAINS * ATT_BLOCK, L)
    kv_spec = lambda part: pl.BlockSpec((1, 1, L, SB_HEAD_DIM), lambda b, h, i: (part, b, 0, h))
    return pl.pallas_call(
        _attn_prompt_kernel,
        grid=(B, SB_HEADS, L // tq),
        in_specs=[
            pl.BlockSpec((1, 1, tq, SB_HEAD_DIM), lambda b, h, i: (0, b, i, h)),
            kv_spec(1),
            kv_spec(2),
        ],
        out_specs=pl.BlockSpec((1, tq, SB_HEAD_DIM), lambda b, h, i: (b, i, h)),
        out_shape=jax.ShapeDtypeStruct((B, L, SB_WIDTH), BF16),
        scratch_shapes=[
            pltpu.VMEM((tq // ATT_BLOCK, ATT_BLOCK, ATT_BLOCK), F32),
            pltpu.VMEM((tq // ATT_BLOCK, ATT_BLOCK, ATT_BLOCK), F32),
        ],
        compiler_params=_params("arbitrary", "arbitrary", "arbitrary"),
        name="attn_prompt",
    )(proj_bf, proj_bf, proj_bf)


def _attn_sample_kernel(q_ref, kn_ref, vn_ref, kc_ref, vc_ref, o_ref):
    L = q_ref.shape[2]
    past = kc_ref.shape[2]
    cblk = min(ATT_BLOCK, past)
    tri_new = _tri_ones(L)
    tri_c = _tri_ones(cblk)
    row = lax.broadcasted_iota(jnp.int32, (L, L), 0)
    col = lax.broadcasted_iota(jnp.int32, (L, L), 1)
    causal = col < row
    lanes = [slice(h * SB_HEAD_DIM, (h + 1) * SB_HEAD_DIM) for h in range(SB_HEADS)]
    qs = [q_ref[0, 0, :, ln] for ln in lanes]
    results = _stick_blocks(qs, [kn_ref[0, 0, :, ln] for ln in lanes], [vn_ref[0, 0, :, ln] for ln in lanes],
                            causal, [jnp.zeros((L, L), F32)] * SB_HEADS, tri_new)
    accs = [out for out, _ in results]
    laters = [jnp.broadcast_to(later[:, 0:1], (L, cblk)) for _, later in results]
    for j in range(past // cblk - 1, -1, -1):
        rows = slice(j * cblk, (j + 1) * cblk)
        results = _stick_blocks(qs, [kc_ref[0, 0, rows, ln].astype(BF16) for ln in lanes],
                                [vc_ref[0, 0, rows, ln].astype(BF16) for ln in lanes], None, laters, tri_c)
        accs = [acc + out for acc, (out, _) in zip(accs, results)]
        laters = [later for _, later in results]
    for ln, acc in zip(lanes, accs):
        o_ref[0, :, ln] = acc.astype(BF16)


def _attn_sample(proj_bf, cache_k, cache_v, layer):
    _, B, L, _ = proj_bf.shape
    past = cache_k.shape[2]
    new_spec = lambda part: pl.BlockSpec((1, 1, L, SB_WIDTH), lambda b: (part, b, 0, 0))
    cache_spec = pl.BlockSpec((1, 1, past, SB_WIDTH), lambda b: (layer, b, 0, 0))
    return pl.pallas_call(
        _attn_sample_kernel,
        grid=(B,),
        in_specs=[new_spec(0), new_spec(1), new_spec(2), cache_spec, cache_spec],
        out_specs=pl.BlockSpec((1, L, SB_WIDTH), lambda b: (b, 0, 0)),
        out_shape=jax.ShapeDtypeStruct((B, L, SB_WIDTH), BF16),
        compiler_params=_params("arbitrary"),
        name="attn_sample",
    )(proj_bf, proj_bf, proj_bf, cache_k, cache_v)


def _outproj_kernel(mix_ref, att_ref, x_ref, g_ref, sc_ref, sh_ref, w_ref, lng_ref, lnb_ref, o_ref, h_ref):
    bb, tl, _ = x_ref.shape
    half = mix_ref.shape[-1]
    for bs, ts in _row_chunks(bb, tl):
        m = mix_ref[bs, ts, :]
        cb, ct, _ = m.shape
        mix = (jnp.dot(m.reshape(cb * ct, half), w_ref[0:half, :], preferred_element_type=F32)
               + jnp.dot(att_ref[bs, ts, :].reshape(cb * ct, half), w_ref[half:2 * half, :],
                         preferred_element_type=F32))
        t = DEEPNORM_ALPHA * x_ref[bs, ts, :] + g_ref[bs] * mix.reshape(cb, ct, D_MODEL)
        x1 = _layer_norm(t) * lng_ref[...] + lnb_ref[...]
        o_ref[bs, ts, :] = x1
        h_ref[bs, ts, :] = (_layer_norm(x1) * (1.0 + sc_ref[bs]) + sh_ref[bs]).astype(BF16)


def _outproj(mix, att, x, gate, sc2, sh2, w_out, ln_g, ln_b, bb, tl):
    B, L, _ = x.shape
    tok = lambda width: pl.BlockSpec((bb, tl, width), lambda b, i: (b, i, 0))
    mod = pl.BlockSpec((bb, 1, D_MODEL), lambda b, i: (b, 0, 0))
    vec = pl.BlockSpec((1, 1, D_MODEL), lambda b, i: (0, 0, 0))
    return pl.pallas_call(
        _outproj_kernel,
        grid=(B // bb, L // tl),
        in_specs=[
            tok(COL_TILE), tok(SB_WIDTH), tok(D_MODEL), mod, mod, mod,
            pl.BlockSpec((2 * COL_TILE, D_MODEL), lambda b, i: (0, 0), pipeline_mode=pl.Buffered(1)),
            vec, vec,
        ],
        out_specs=[tok(D_MODEL), tok(D_MODEL)],
        out_shape=[jax.ShapeDtypeStruct((B, L, D_MODEL), F32), jax.ShapeDtypeStruct((B, L, D_MODEL), BF16)],
        compiler_params=_params("arbitrary", "arbitrary"),
        name="outproj",
    )(mix, att, x, gate, sc2, sh2, w_out, ln_g.reshape(1, 1, D_MODEL), ln_b.reshape(1, 1, D_MODEL))


def _ffn_kernel(h_ref, x_ref, g_ref, wup_ref, wdn_ref, lng_ref, lnb_ref, o_ref, acc_ref):
    f = pl.program_id(2)
    bb, tl, _ = x_ref.shape
    rows = bb * tl
    hid = jnp.dot(h_ref[...].reshape(rows, D_MODEL), wup_ref[...], preferred_element_type=F32)
    hid = jnp.square(jnp.maximum(hid, 0.0))
    part = jnp.dot(hid.astype(BF16), wdn_ref[...], preferred_element_type=F32)
    acc_ref[...] = jnp.where(f > 0, acc_ref[...], 0.0) + part

    @pl.when(f == pl.num_programs(2) - 1)
    def _():
        t = DEEPNORM_ALPHA * x_ref[...] + g_ref[...] * acc_ref[...].reshape(bb, tl, D_MODEL)
        o_ref[...] = _layer_norm(t) * lng_ref[...] + lnb_ref[...]


def _ffn(h2, x, gate, w_up, w_down, ln_g, ln_b, bb, tl):
    B, L, _ = x.shape
    tf = FF_TILE
    tok = pl.BlockSpec((bb, tl, D_MODEL), lambda b, i, f: (b, i, 0))
    mod = pl.BlockSpec((bb, 1, D_MODEL), lambda b, i, f: (b, 0, 0))
    vec = pl.BlockSpec((1, 1, D_MODEL), lambda b, i, f: (0, 0, 0))
    return pl.pallas_call(
        _ffn_kernel,
        grid=(B // bb, L // tl, D_FF // tf),
        in_specs=[
            tok, tok, mod,
            pl.BlockSpec((D_MODEL, tf), lambda b, i, f: (0, f)),
            pl.BlockSpec((tf, D_MODEL), lambda b, i, f: (f, 0)),
            vec, vec,
        ],
        out_specs=tok,
        out_shape=jax.ShapeDtypeStruct((B, L, D_MODEL), F32),
        scratch_shapes=[pltpu.VMEM((bb * tl, D_MODEL), F32)],
        compiler_params=_params("arbitrary", "arbitrary", "arbitrary"),
        name="ffn",
    )(h2, x, gate, w_up, w_down, ln_g.reshape(1, 1, D_MODEL), ln_b.reshape(1, 1, D_MODEL))


def _row_blocking(B, L):
    tl = min(L, ROW_TILE)
    bb = max(1, min(B, ROW_TILE // tl))
    return bb, tl


def _run_group(x, ada, hist, s5_re, s5_im, cache_k, cache_v, pos0, lw, s5_raw):
    B, L, _ = x.shape
    bb, tl = _row_blocking(B, L)
    chunk = min(L, S5_CHUNK)
    tab, _ = _s5prep(*s5_raw, chunk // SUBLANES)
    new_hist, new_re, new_im = [], [], []
    kv = None
    for l in range(DEPTH):
        w = lw[l]
        sh1, sc1, g1, sh2, sc2, g2 = [ada[l][:, None, n * D_MODEL:(n + 1) * D_MODEL] for n in range(6)]
        pu, k_all, v_all, proj_bf = _inproj(x, sc1, sh1, w["w_in"], kv, l, bb, tl)
        kv = (k_all, v_all)
        hist16 = jnp.pad(hist[l], ((0, 0), (HIST_ROWS - POOL_HIST, 0), (0, 0)))
        h0 = jnp.concatenate([s5_re[l].reshape(B, 1, S5_FLAT), s5_im[l].reshape(B, 1, S5_FLAT)], axis=-1)
        h0 = jnp.broadcast_to(h0, (B, SUBLANES, 2 * S5_FLAT))
        mix, nh, ns = _pools5(pu, hist16, h0, w["w_pool"], w["pool_scale"], tab[l:l + 1], w["b_blk"],
                              w["c_blk"], w["d_skip"], w["w_glu"], pos0, chunk)
        if cache_k is None:
            att = _attn_prompt(proj_bf)
        else:
            att = _attn_sample(proj_bf, cache_k, cache_v, l)
        x, h2 = _outproj(mix, att, x, g1, sc2, sh2, w["w_out"], w["ln1_g"], w["ln1_b"], bb, tl)
        x = _ffn(h2, x, g2, w["w_up"], w["w_down"], w["ln2_g"], w["ln2_b"], bb, tl)
        new_hist.append(nh[:, HIST_ROWS - POOL_HIST:, :])
        new_re.append(ns[:, 0, 0:S5_FLAT].reshape(B, S5_GROUPS, S5_STATE))
        new_im.append(ns[:, 0, S5_FLAT:].reshape(B, S5_GROUPS, S5_STATE))
    stack = lambda xs: jnp.stack(xs)
    heads = lambda a: a.reshape(DEPTH, B, L, SB_HEADS, SB_HEAD_DIM)
    return x, (stack(new_hist), stack(new_re), stack(new_im), heads(kv[0]), heads(kv[1]))


def kernel(x_prompt, x_sample, state_pool, state_s5_re, state_s5_im, cache_k, cache_v, c_prompt, c_sample,
           w_ada, b_ada, w_in, w_pool, pool_scale, s5_a_re, s5_a_im, s5_log_dt, s5_b_re, s5_b_im,
           s5_c_re, s5_c_im, s5_d, w_glu, w_out, ln1_g, ln1_b, w_up, w_down, ln2_g, ln2_b):
    B = x_prompt.shape[0]
    Bs = x_sample.shape[0]
    past = cache_k.shape[2]

    n_c = B + Bs
    pad = (-n_c) % 16
    c_all = jnp.concatenate([c_prompt, c_sample, jnp.zeros((pad, D_MODEL), F32)], axis=0)
    ada = _ada(c_all, w_ada, b_ada)
    ada_p, ada_s = ada[:, 0:B], ada[:, B:n_c]

    s5_raw = (s5_a_re, s5_a_im, s5_log_dt, s5_b_re, s5_b_im)
    _, bb_disc = _s5prep(*s5_raw, SUBLANES)
    lw = []
    for l in range(DEPTH):
        b_blk, c_blk = _s5_block_weights(bb_disc[l], s5_c_re[l], s5_c_im[l])
        lw.append(dict(
            w_in=w_in[l].astype(BF16), w_out=w_out[l].astype(BF16),
            w_up=w_up[l].astype(BF16), w_down=w_down[l].astype(BF16),
            w_glu=w_glu[l].astype(BF16), w_pool=w_pool[l].astype(BF16),
            pool_scale=pool_scale[l].reshape(1, POOL_WIDTH),
            d_skip=s5_d[l].reshape(1, S5_WIDTH),
            b_blk=b_blk, c_blk=c_blk,
            ln1_g=ln1_g[l], ln1_b=ln1_b[l], ln2_g=ln2_g[l], ln2_b=ln2_b[l],
        ))

    zero_hist = jnp.zeros((DEPTH, B, POOL_HIST, POOL_WIDTH), F32)
    zero_s5 = jnp.zeros((DEPTH, B, S5_GROUPS, S5_STATE), F32)
    y_p, (pool_p, re_p, im_p, k_p, v_p) = _run_group(
        x_prompt, ada_p, zero_hist, zero_s5, zero_s5, None, None, 0, lw, s5_raw)
    ck = cache_k.reshape(DEPTH, Bs, past, SB_WIDTH)
    cv = cache_v.reshape(DEPTH, Bs, past, SB_WIDTH)
    y_s, (pool_s, re_s, im_s, k_s, v_s) = _run_group(
        x_sample, ada_s, state_pool, state_s5_re, state_s5_im, ck, cv, past, lw, s5_raw)
    return (y_p, y_s, pool_p, re_p, im_p, k_p, v_p, pool_s, re_s, im_s, k_s, v_s)
```

```python
import functools
import math

import jax
import jax.numpy as jnp
from jax import lax
from jax.experimental import pallas as pl
from jax.experimental.pallas import tpu as pltpu

F32 = jnp.float32
BF16 = jnp.bfloat16

D_MODEL = 2048
DEPTH = 2
POOL_WIDTH = 512
POOL_WINDOWS = (2, 4, 8, 16)
POOL_GROUP = 128
POOL_HIST = 15
HIST_ROWS = 16
S5_WIDTH = 512
S5_GROUP_CH = 16
S5_GROUPS = 32
S5_STATE = 64
S5_FLAT = S5_GROUPS * S5_STATE
SB_WIDTH = 1024
SB_HEAD_DIM = 128
SB_HEADS = 8
D_FF = 4 * D_MODEL
IN_WIDTH = 4096
COL_TILE = 1024
DEEPNORM_ALPHA = (2 * DEPTH) ** 0.25
LN_EPS = 1e-5

V7X_VMEM_LIMIT = 56 * 1024 * 1024
SUBLANES = 8
ROW_TILE = 512
ROW_CHUNKS = 4
FF_TILE = 1024
S5_CHUNK = 256
S5_LANE_CHUNK = 512
S5_BLOCKS = S5_FLAT // S5_LANE_CHUNK
S5_BLOCK_CH = S5_WIDTH // S5_BLOCKS
S5_TABLES = 12
ATT_BLOCK = 128
ATT_CHAINS = 16
LOG2_E = 1.4426950408889634
STICK_LOG2_FLOOR = -150.0
STICK_PARKED = -1e30


def _params(*sem):
    return pltpu.CompilerParams(dimension_semantics=sem, vmem_limit_bytes=V7X_VMEM_LIMIT)


def _layer_norm(x):
    mu = jnp.mean(x, axis=-1, keepdims=True)
    xc = x - mu
    var = jnp.mean(xc * xc, axis=-1, keepdims=True)
    return xc * lax.rsqrt(var + LN_EPS)


def _row_chunks(bb, tl):
    if bb >= ROW_CHUNKS:
        step = bb // ROW_CHUNKS
        return [(slice(c * step, (c + 1) * step), slice(0, tl)) for c in range(ROW_CHUNKS)]
    step = tl // ROW_CHUNKS
    return [(slice(0, bb), slice(c * step, (c + 1) * step)) for c in range(ROW_CHUNKS)]


def _cmul(a_re, a_im, b_re, b_im):
    return a_re * b_re - a_im * b_im, a_re * b_im + a_im * b_re


def _ada_kernel(c_ref, w_ref, b_ref, o_ref):
    c = c_ref[...]
    s = (c * jax.nn.sigmoid(c)).astype(BF16)
    w = w_ref[0].astype(BF16)
    o_ref[0] = jnp.dot(s, w, preferred_element_type=F32) + b_ref[0]


def _ada(c_all, w_ada, b_ada):
    rows = c_all.shape[0]
    n_out = w_ada.shape[-1]
    tn = 1024
    return pl.pallas_call(
        _ada_kernel,
        grid=(DEPTH, n_out // tn),
        in_specs=[
            pl.BlockSpec((rows, D_MODEL), lambda l, j: (0, 0)),
            pl.BlockSpec((1, D_MODEL, tn), lambda l, j: (l, 0, j)),
            pl.BlockSpec((1, 1, tn), lambda l, j: (l, 0, j)),
        ],
        out_specs=pl.BlockSpec((1, rows, tn), lambda l, j: (l, 0, j)),
        out_shape=jax.ShapeDtypeStruct((DEPTH, rows, n_out), F32),
        compiler_params=_params("arbitrary", "arbitrary"),
        name="ada",
    )(c_all, w_ada, b_ada.reshape(DEPTH, 1, n_out))


def _s5prep_kernel(seg, are_ref, aim_ref, ldt_ref, bre_ref, bim_ref, tab_ref, bb_ref):
    a_re = are_ref[0]
    a_im = aim_ref[0]
    dt = jnp.exp(ldt_ref[0])
    shape = (SUBLANES, S5_FLAT)
    row = lax.broadcasted_iota(jnp.int32, shape, 0)
    mag = jnp.broadcast_to(jnp.exp(a_re * dt), shape)
    ang = jnp.broadcast_to(a_im * dt, shape)
    ab_re = mag * jnp.cos(ang)
    ab_im = mag * jnp.sin(ang)
    tab_ref[0, 0] = ab_re
    tab_ref[0, 1] = ab_im
    p_re, p_im = ab_re, ab_im
    for _ in range(seg.bit_length() - 1):
        p_re, p_im = _cmul(p_re, p_im, p_re, p_im)
    ps_re = jnp.ones(shape, F32)
    ps_im = jnp.zeros(shape, F32)
    for n, k in enumerate((1, 2, 4)):
        tab_ref[0, 2 + 2 * n] = jnp.where(row >= k, p_re, 0.0)
        tab_ref[0, 3 + 2 * n] = jnp.where(row >= k, p_im, 0.0)
        q_re, q_im = _cmul(ps_re, ps_im, p_re, p_im)
        has_bit = (row & k) != 0
        ps_re = jnp.where(has_bit, q_re, ps_re)
        ps_im = jnp.where(has_bit, q_im, ps_im)
        p_re, p_im = _cmul(p_re, p_im, p_re, p_im)
    tab_ref[0, 8] = ps_re
    tab_ref[0, 9] = ps_im
    tab_ref[0, 10] = p_re
    tab_ref[0, 11] = p_im
    a1_re = ab_re[0:1, :]
    a1_im = ab_im[0:1, :]
    den = a_re * a_re + a_im * a_im
    f_re = ((a1_re - 1.0) * a_re + a1_im * a_im) / den
    f_im = (a1_im * a_re - (a1_re - 1.0) * a_im) / den
    b_re = bre_ref[0]
    b_im = bim_ref[0]
    bb_ref[0, 0] = f_re * b_re - f_im * b_im
    bb_ref[0, 1] = f_re * b_im + f_im * b_re


def _s5prep(a_re, a_im, log_dt, b_re, b_im, seg):
    assert seg & (seg - 1) == 0
    flat = lambda a: a.reshape(DEPTH, 1, S5_FLAT)
    ldt = jnp.broadcast_to(log_dt[:, :, None], (DEPTH, S5_GROUPS, S5_STATE)).reshape(DEPTH, 1, S5_FLAT)
    to_ch_major = lambda b: jnp.transpose(b, (0, 3, 1, 2)).reshape(DEPTH, S5_GROUP_CH, S5_FLAT)
    row_spec = pl.BlockSpec((1, 1, S5_FLAT), lambda l: (l, 0, 0))
    b_spec = pl.BlockSpec((1, S5_GROUP_CH, S5_FLAT), lambda l: (l, 0, 0))
    return pl.pallas_call(
        functools.partial(_s5prep_kernel, seg),
        grid=(DEPTH,),
        in_specs=[row_spec, row_spec, row_spec, b_spec, b_spec],
        out_specs=[
            pl.BlockSpec((1, S5_TABLES, SUBLANES, S5_FLAT), lambda l: (l, 0, 0, 0)),
            pl.BlockSpec((1, 2, S5_GROUP_CH, S5_FLAT), lambda l: (l, 0, 0, 0)),
        ],
        out_shape=[
            jax.ShapeDtypeStruct((DEPTH, S5_TABLES, SUBLANES, S5_FLAT), F32),
            jax.ShapeDtypeStruct((DEPTH, 2, S5_GROUP_CH, S5_FLAT), F32),
        ],
        compiler_params=_params("arbitrary"),
        name="s5prep",
    )(flat(a_re), flat(a_im), ldt, to_ch_major(b_re), to_ch_major(b_im))


def _s5_block_weights(bb, c_re, c_im):
    ch_group = jnp.arange(S5_WIDTH)[:, None] // S5_GROUP_CH
    st_group = jnp.arange(S5_FLAT)[None, :] // S5_STATE
    same = ch_group == st_group
    ch_b, st_b = S5_BLOCK_CH, S5_LANE_CHUNK

    def b_blocks(b):
        dense = jnp.where(same, jnp.tile(b, (S5_GROUPS, 1)), 0.0)
        return jnp.stack([dense[m * ch_b:(m + 1) * ch_b, m * st_b:(m + 1) * st_b] for m in range(S5_BLOCKS)])

    def c_blocks(c):
        c_sp = jnp.transpose(c, (0, 2, 1)).reshape(S5_FLAT, S5_GROUP_CH)
        dense = jnp.where(same.T, jnp.tile(c_sp, (1, S5_GROUPS)), 0.0)
        return jnp.stack([dense[m * st_b:(m + 1) * st_b, m * ch_b:(m + 1) * ch_b] for m in range(S5_BLOCKS)])

    b_blk = jnp.stack([b_blocks(bb[0]), b_blocks(bb[1])]).astype(BF16)
    c_blk = jnp.stack([c_blocks(c_re), c_blocks(c_im)]).astype(BF16)
    return b_blk, c_blk


def _inproj_kernel(x_ref, sc_ref, sh_ref, w_ref, *rest):
    pu_ref, k_ref, v_ref, bf_ref = rest[-4:]
    bb, tl, _ = x_ref.shape
    for bs, ts in _row_chunks(bb, tl):
        x = x_ref[bs, ts, :]
        cb, ct, _ = x.shape
        h = (_layer_norm(x) * (1.0 + sc_ref[bs]) + sh_ref[bs]).reshape(cb * ct, D_MODEL).astype(BF16)

        def cols(j, h=h, cb=cb, ct=ct):
            r = jnp.dot(h, w_ref[:, j * COL_TILE:(j + 1) * COL_TILE], preferred_element_type=F32)
            return r.reshape(cb, ct, COL_TILE)

        pu_ref[bs, ts, :] = cols(0)
        bf_ref[0, bs, ts, :] = cols(1).astype(BF16)
        k = cols(2)
        k_ref[0, bs, ts, :, :] = pltpu.einshape("bt(hd)->bthd", k, h=SB_HEADS)
        bf_ref[1, bs, ts, :] = k.astype(BF16)
        v = cols(3)
        v_ref[0, bs, ts, :, :] = pltpu.einshape("bt(hd)->bthd", v, h=SB_HEADS)
        bf_ref[2, bs, ts, :] = v.astype(BF16)


def _inproj(x, sc, sh, w_in, kv_prev, layer, bb, tl):
    B, L, _ = x.shape
    tok = lambda b, i: (b, i, 0)
    in_specs = [
        pl.BlockSpec((bb, tl, D_MODEL), tok),
        pl.BlockSpec((bb, 1, D_MODEL), lambda b, i: (b, 0, 0)),
        pl.BlockSpec((bb, 1, D_MODEL), lambda b, i: (b, 0, 0)),
        pl.BlockSpec((D_MODEL, IN_WIDTH), lambda b, i: (0, 0), pipeline_mode=pl.Buffered(1)),
    ]
    args = [x, sc, sh, w_in]
    aliases = {}
    if kv_prev is not None:
        in_specs += [pl.BlockSpec(memory_space=pl.ANY)] * 2
        args += list(kv_prev)
        aliases = {4: 1, 5: 2}
    kv_spec = pl.BlockSpec((1, bb, tl, SB_HEADS, SB_HEAD_DIM), lambda b, i: (layer, b, i, 0, 0))
    kv_shape = jax.ShapeDtypeStruct((DEPTH, B, L, SB_HEADS, SB_HEAD_DIM), F32)
    return pl.pallas_call(
        _inproj_kernel,
        grid=(B // bb, L // tl),
        in_specs=in_specs,
        out_specs=[
            pl.BlockSpec((bb, tl, COL_TILE), tok),
            kv_spec, kv_spec,
            pl.BlockSpec((3, bb, tl, COL_TILE), lambda b, i: (0, b, i, 0)),
        ],
        out_shape=[
            jax.ShapeDtypeStruct((B, L, COL_TILE), F32),
            kv_shape, kv_shape,
            jax.ShapeDtypeStruct((3, B, L, COL_TILE), BF16),
        ],
        input_output_aliases=aliases,
        compiler_params=_params("arbitrary", "arbitrary"),
        name="inproj",
    )(*args)


def _gelu_tanh(y):
    return 0.5 * y * (1.0 + jnp.tanh(math.sqrt(2.0 / math.pi) * (y + 0.044715 * (y * y * y))))


def _pools5_kernel(pos0, pu_ref, hist_ref, h0_ref, wpool_ref, pscale_ref, tab_ref, bblk_ref,
                   cblk_ref, dskip_ref, wglu_ref, mix_ref, nhist_ref, nstate_ref,
                   ext_ref, bu_ref, carry_ref):
    i = pl.program_id(1)
    T = pu_ref.shape[1]

    @pl.when(i == 0)
    def _():
        ext_ref[0:HIST_ROWS, :] = hist_ref[0]
        carry_ref[...] = h0_ref[0]

    p = pu_ref[0, :, 0:POOL_WIDTH]
    ext_ref[HIST_ROWS:HIST_ROWS + T, :] = p
    pos = pos0 + i * T + lax.broadcasted_iota(jnp.int32, (T, POOL_GROUP), 0)
    for g, w in enumerate(POOL_WINDOWS):
        lo, hi = g * POOL_GROUP, (g + 1) * POOL_GROUP
        win = ext_ref[HIST_ROWS:HIST_ROWS + T, lo:hi]
        for k in range(1, w):
            win = win + ext_ref[HIST_ROWS - k:HIST_ROWS - k + T, lo:hi]
        cnt = jnp.minimum(w, pos + 1).astype(F32)
        mixed = win / cnt - ext_ref[HIST_ROWS:HIST_ROWS + T, lo:hi]
        out = jnp.dot(mixed.astype(BF16), wpool_ref[g], preferred_element_type=F32)
        mix_ref[0, :, lo:hi] = (out * pscale_ref[:, lo:hi]).astype(BF16)
    last = ext_ref[T:T + HIST_ROWS, :]
    nhist_ref[0] = last
    ext_ref[0:HIST_ROWS, :] = last

    seg = T // SUBLANES
    shift = seg.bit_length() - 1
    n_idx = lax.broadcasted_iota(jnp.int32, (T, T), 0)
    t_idx = lax.broadcasted_iota(jnp.int32, (T, T), 1)
    perm = jnp.where(t_idx == (n_idx & (SUBLANES - 1)) * seg + (n_idx >> 3), 1.0, 0.0).astype(BF16)
    unperm = jnp.where(t_idx == (n_idx & (seg - 1)) * SUBLANES + (n_idx >> shift), 1.0, 0.0).astype(BF16)

    u = pu_ref[0, :, POOL_WIDTH:POOL_WIDTH + S5_WIDTH]
    u_hi = u.astype(BF16)
    u_lo = (u - u_hi.astype(F32)).astype(BF16)
    up_hi = jnp.dot(perm, u_hi, preferred_element_type=F32)
    u_perm = up_hi + jnp.dot(perm, u_lo, preferred_element_type=F32)
    ub = up_hi.astype(BF16)

    lc = S5_LANE_CHUNK
    row8 = lax.broadcasted_iota(jnp.int32, (SUBLANES, lc), 0)
    y_parts = []
    for m in range(S5_BLOCKS):
        re_l = slice(m * lc, (m + 1) * lc)
        im_l = slice(S5_FLAT + m * lc, S5_FLAT + (m + 1) * lc)
        ch_l = slice(m * S5_BLOCK_CH, (m + 1) * S5_BLOCK_CH)
        bu_ref[:, re_l] = jnp.dot(ub[:, ch_l], bblk_ref[0, m], preferred_element_type=F32)
        bu_ref[:, im_l] = jnp.dot(ub[:, ch_l], bblk_ref[1, m], preferred_element_type=F32)
        a_re = tab_ref[0, 0, :, re_l]
        a_im = tab_ref[0, 1, :, re_l]

        f_re = f_im = jnp.zeros((SUBLANES, lc), F32)
        for i in range(seg):
            rows = slice(i * SUBLANES, (i + 1) * SUBLANES)
            g_re, g_im = _cmul(a_re, a_im, f_re, f_im)
            f_re = g_re + bu_ref[rows, re_l]
            f_im = g_im + bu_ref[rows, im_l]
            bu_ref[rows, re_l] = f_re
            bu_ref[rows, im_l] = f_im

        for n, k in enumerate((1, 2, 4)):
            d_re, d_im = _cmul(tab_ref[0, 2 + 2 * n, :, re_l], tab_ref[0, 3 + 2 * n, :, re_l],
                               pltpu.roll(f_re, k, 0), pltpu.roll(f_im, k, 0))
            f_re, f_im = f_re + d_re, f_im + d_im
        c_re = carry_ref[:, re_l]
        c_im = carry_ref[:, im_l]
        e_re, e_im = _cmul(tab_ref[0, 8, :, re_l], tab_ref[0, 9, :, re_l], c_re, c_im)
        e_re = e_re + jnp.where(row8 >= 1, pltpu.roll(f_re, 1, 0), 0.0)
        e_im = e_im + jnp.where(row8 >= 1, pltpu.roll(f_im, 1, 0), 0.0)
        n_re, n_im = _cmul(tab_ref[0, 10, :, re_l], tab_ref[0, 11, :, re_l], c_re, c_im)
        carry_ref[:, re_l] = n_re + jnp.broadcast_to(f_re[SUBLANES - 1:SUBLANES, :], (SUBLANES, lc))
        carry_ref[:, im_l] = n_im + jnp.broadcast_to(f_im[SUBLANES - 1:SUBLANES, :], (SUBLANES, lc))

        w_re, w_im = _cmul(a_re, a_im, e_re, e_im)
        for i in range(seg):
            rows = slice(i * SUBLANES, (i + 1) * SUBLANES)
            bu_ref[rows, re_l] = bu_ref[rows, re_l] + w_re
            bu_ref[rows, im_l] = bu_ref[rows, im_l] + w_im
            if i + 1 < seg:
                w_re, w_im = _cmul(a_re, a_im, w_re, w_im)

        y_parts.append(
            jnp.dot(bu_ref[:, re_l].astype(BF16), cblk_ref[0, m], preferred_element_type=F32)
            - jnp.dot(bu_ref[:, im_l].astype(BF16), cblk_ref[1, m], preferred_element_type=F32))

    nstate_ref[0] = carry_ref[0:1, :]
    y = _gelu_tanh(jnp.concatenate(y_parts, axis=1) + dskip_ref[...] * u_perm)
    gate = jnp.dot(y.astype(BF16), wglu_ref[...], preferred_element_type=F32)
    s5_perm = (y * jax.nn.sigmoid(gate)).astype(BF16)
    mix_ref[0, :, POOL_WIDTH:POOL_WIDTH + S5_WIDTH] = jnp.dot(
        unperm, s5_perm, preferred_element_type=F32).astype(BF16)


def _pools5(pu, hist, h0, w_pool, pool_scale, tab, b_blk, c_blk, d_skip, w_glu, pos0, chunk):
    B, L, _ = pu.shape
    T = chunk
    const2 = lambda b, i: (0, 0)
    const4 = lambda b, i: (0, 0, 0, 0)
    return pl.pallas_call(
        functools.partial(_pools5_kernel, pos0),
        grid=(B, L // T),
        in_specs=[
            pl.BlockSpec((1, T, COL_TILE), lambda b, i: (b, i, 0)),
            pl.BlockSpec((1, HIST_ROWS, POOL_WIDTH), lambda b, i: (b, 0, 0)),
            pl.BlockSpec((1, SUBLANES, 2 * S5_FLAT), lambda b, i: (b, 0, 0)),
            pl.BlockSpec((len(POOL_WINDOWS), POOL_GROUP, POOL_GROUP), lambda b, i: (0, 0, 0)),
            pl.BlockSpec((1, POOL_WIDTH), const2),
            pl.BlockSpec((1, S5_TABLES, SUBLANES, S5_FLAT), const4),
            pl.BlockSpec((2, S5_BLOCKS, S5_BLOCK_CH, S5_LANE_CHUNK), const4),
            pl.BlockSpec((2, S5_BLOCKS, S5_LANE_CHUNK, S5_BLOCK_CH), const4),
            pl.BlockSpec((1, S5_WIDTH), const2),
            pl.BlockSpec((S5_WIDTH, S5_WIDTH), const2),
        ],
        out_specs=[
            pl.BlockSpec((1, T, COL_TILE), lambda b, i: (b, i, 0)),
            pl.BlockSpec((1, HIST_ROWS, POOL_WIDTH), lambda b, i: (b, 0, 0)),
            pl.BlockSpec((1, 1, 2 * S5_FLAT), lambda b, i: (b, 0, 0)),
        ],
        out_shape=[
            jax.ShapeDtypeStruct((B, L, COL_TILE), BF16),
            jax.ShapeDtypeStruct((B, HIST_ROWS, POOL_WIDTH), F32),
            jax.ShapeDtypeStruct((B, 1, 2 * S5_FLAT), F32),
        ],
        scratch_shapes=[
            pltpu.VMEM((HIST_ROWS + T, POOL_WIDTH), F32),
            pltpu.VMEM((T, 2 * S5_FLAT), F32),
            pltpu.VMEM((SUBLANES, 2 * S5_FLAT), F32),
        ],
        compiler_params=_params("arbitrary", "arbitrary"),
        name="pools5",
    )(pu, hist, h0, w_pool, pool_scale, tab, b_blk, c_blk, d_skip, w_glu)


def _tri_ones(n):
    j = lax.broadcasted_iota(jnp.int32, (2 * n, 2 * n), 0)
    j = jnp.where(j >= n, j - n, j)
    s = lax.broadcasted_iota(jnp.int32, (2 * n, 2 * n), 1)
    return jnp.where((j >= s) | (s >= n), 1.0, 0.0).astype(BF16)


def _stick_blocks(qs, ks, vs, mask, laters, tri):
    n = ks[0].shape[0]
    zs = [lax.dot_general(q, k, (((1,), (1,)), ((), ())), preferred_element_type=F32)
          * (SB_HEAD_DIM ** -0.5 * LOG2_E) for q, k in zip(qs, ks)]
    sums = []
    for z in zs:
        sp = jnp.maximum(z, 0.0) + jnp.log2(1.0 + jnp.exp2(-jnp.abs(z)))
        if mask is not None:
            sp = jnp.where(mask, sp, 0.0)
        sp_hi = sp.astype(BF16)
        sp_lo = (sp - sp_hi.astype(F32)).astype(BF16)
        sums.append(jnp.dot(jnp.concatenate([sp_hi, sp_lo], axis=1), tri, preferred_element_type=F32))
    results = []
    for z, s, v, later in zip(zs, sums, vs, laters):
        w = jnp.exp2(z + (later - s[:, 0:n]))
        if mask is not None:
            w = jnp.where(mask, w, 0.0)
        results.append((jnp.dot(w.astype(BF16), v, preferred_element_type=F32), later - s[:, n:2 * n]))
    return results


def _attn_prompt_kernel(q_ref, k_ref, v_ref, o_ref, acc_ref, later_ref):
    tile = pl.program_id(2)
    blk = ATT_BLOCK
    n_chain = q_ref.shape[2] // blk
    tri = _tri_ones(blk)
    row = lax.broadcasted_iota(jnp.int32, (blk, blk), 0)
    col = lax.broadcasted_iota(jnp.int32, (blk, blk), 1)
    causal = col < row
    zeros = jnp.zeros((blk, blk), F32)

    def key_rows(kb):
        return pl.ds(pl.multiple_of(kb * blk, blk), blk)

    def chain_q():
        return [q_ref[0, 0, c * blk:(c + 1) * blk, :] for c in range(n_chain)]

    rows = [key_rows(tile * n_chain + c) for c in range(n_chain)]
    results = _stick_blocks(chain_q(), [k_ref[0, 0, r, :] for r in rows], [v_ref[0, 0, r, :] for r in rows],
                            causal, [zeros] * n_chain, tri)
    alive = None
    for c, (out, later) in enumerate(results):
        acc_ref[c] = out
        later_ref[c] = later
        alive = later if alive is None else jnp.maximum(alive, later)

    def cond(carry):
        t, go = carry
        return jnp.logical_and(t <= tile * n_chain + (n_chain - 1), go > 0)

    def body(carry):
        t, _ = carry
        kbs = [tile * n_chain + c - t for c in range(n_chain)]
        laters = [jnp.where(kb < 0, STICK_PARKED, later_ref[c]) for c, kb in enumerate(kbs)]
        rows = [key_rows(jnp.maximum(kb, 0)) for kb in kbs]
        results = _stick_blocks(chain_q(), [k_ref[0, 0, r, :] for r in rows],
                                [v_ref[0, 0, r, :] for r in rows], None, laters, tri)
        alive = None
        for c, (out, later) in enumerate(results):
            acc_ref[c] += out
            later_ref[c] = later
            alive = later if alive is None else jnp.maximum(alive, later)
        return t + 1, (jnp.max(alive) >= STICK_LOG2_FLOOR).astype(jnp.int32)

    go = (jnp.max(alive) >= STICK_LOG2_FLOOR).astype(jnp.int32)
    lax.while_loop(cond, body, (jnp.int32(1), go))
    for c in range(n_chain):
        o_ref[0, c * blk:(c + 1) * blk, :] = acc_ref[c].astype(BF16)


def _attn_prompt(proj_bf):
    _, B, L, _ = proj_bf.shape
    tq = min(ATT_CHAINS * ATT_BLOCK, L)
    kv_spec = lambda part: pl.BlockSpec((1, 1, L, SB_HEAD_DIM), lambda b, h, i: (part, b, 0, h))
    return pl.pallas_call(
        _attn_prompt_kernel,
        grid=(B, SB_HEADS, L // tq),
        in_specs=[
            pl.BlockSpec((1, 1, tq, SB_HEAD_DIM), lambda b, h, i: (0, b, i, h)),
            kv_spec(1),
            kv_spec(2),
        ],
        out_specs=pl.BlockSpec((1, tq, SB_HEAD_DIM), lambda b, h, i: (b, i, h)),
        out_shape=jax.ShapeDtypeStruct((B, L, SB_WIDTH), BF16),
        scratch_shapes=[
            pltpu.VMEM((tq // ATT_BLOCK, ATT_BLOCK, ATT_BLOCK), F32),
            pltpu.VMEM((tq // ATT_BLOCK, ATT_BLOCK, ATT_BLOCK), F32),
        ],
        compiler_params=_params("arbitrary", "arbitrary", "arbitrary"),
        name="attn_prompt",
    )(proj_bf, proj_bf, proj_bf)


def _attn_sample_kernel(q_ref, kn_ref, vn_ref, kc_ref, vc_ref, o_ref):
    L = q_ref.shape[2]
    past = kc_ref.shape[2]
    cblk = min(ATT_BLOCK, past)
    tri_new = _tri_ones(L)
    tri_c = _tri_ones(cblk)
    row = lax.broadcasted_iota(jnp.int32, (L, L), 0)
    col = lax.broadcasted_iota(jnp.int32, (L, L), 1)
    causal = col < row
    lanes = [slice(h * SB_HEAD_DIM, (h + 1) * SB_HEAD_DIM) for h in range(SB_HEADS)]
    qs = [q_ref[0, 0, :, ln] for ln in lanes]
    results = _stick_blocks(qs, [kn_ref[0, 0, :, ln] for ln in lanes], [vn_ref[0, 0, :, ln] for ln in lanes],
                            causal, [jnp.zeros((L, L), F32)] * SB_HEADS, tri_new)
    accs = [out for out, _ in results]
    laters = [jnp.broadcast_to(later[:, 0:1], (L, cblk)) for _, later in results]
    for j in range(past // cblk - 1, -1, -1):
        rows = slice(j * cblk, (j + 1) * cblk)
        kc = pltpu.einshape("phd->p(hd)", kc_ref[0, 0, rows, :, :]).astype(BF16)
        vc = pltpu.einshape("phd->p(hd)", vc_ref[0, 0, rows, :, :]).astype(BF16)
        results = _stick_blocks(qs, [kc[:, ln] for ln in lanes], [vc[:, ln] for ln in lanes],
                                None, laters, tri_c)
        accs = [acc + out for acc, (out, _) in zip(accs, results)]
        laters = [later for _, later in results]
    for ln, acc in zip(lanes, accs):
        o_ref[0, :, ln] = acc.astype(BF16)


def _attn_sample(proj_bf, cache_k, cache_v, layer):
    _, B, L, _ = proj_bf.shape
    past = cache_k.shape[2]
    new_spec = lambda part: pl.BlockSpec((1, 1, L, SB_WIDTH), lambda b: (part, b, 0, 0))
    cache_spec = pl.BlockSpec((1, 1, past, SB_HEADS, SB_HEAD_DIM), lambda b: (layer, b, 0, 0, 0))
    return pl.pallas_call(
        _attn_sample_kernel,
        grid=(B,),
        in_specs=[new_spec(0), new_spec(1), new_spec(2), cache_spec, cache_spec],
        out_specs=pl.BlockSpec((1, L, SB_WIDTH), lambda b: (b, 0, 0)),
        out_shape=jax.ShapeDtypeStruct((B, L, SB_WIDTH), BF16),
        compiler_params=_params("arbitrary"),
        name="attn_sample",
    )(proj_bf, proj_bf, proj_bf, cache_k, cache_v)


def _outproj_kernel(mix_ref, att_ref, x_ref, g_ref, sc_ref, sh_ref, w_ref, lng_ref, lnb_ref, o_ref, h_ref):
    bb, tl, _ = x_ref.shape
    half = mix_ref.shape[-1]
    for bs, ts in _row_chunks(bb, tl):
        m = mix_ref[bs, ts, :]
        cb, ct, _ = m.shape
        mix = (jnp.dot(m.reshape(cb * ct, half), w_ref[0:half, :], preferred_element_type=F32)
               + jnp.dot(att_ref[bs, ts, :].reshape(cb * ct, half), w_ref[half:2 * half, :],
                         preferred_element_type=F32))
        t = DEEPNORM_ALPHA * x_ref[bs, ts, :] + g_ref[bs] * mix.reshape(cb, ct, D_MODEL)
        x1 = _layer_norm(t) * lng_ref[...] + lnb_ref[...]
        o_ref[bs, ts, :] = x1
        h_ref[bs, ts, :] = (_layer_norm(x1) * (1.0 + sc_ref[bs]) + sh_ref[bs]).astype(BF16)


def _outproj(mix, att, x, gate, sc2, sh2, w_out, ln_g, ln_b, bb, tl):
    B, L, _ = x.shape
    tok = lambda width: pl.BlockSpec((bb, tl, width), lambda b, i: (b, i, 0))
    mod = pl.BlockSpec((bb, 1, D_MODEL), lambda b, i: (b, 0, 0))
    vec = pl.BlockSpec((1, 1, D_MODEL), lambda b, i: (0, 0, 0))
    return pl.pallas_call(
        _outproj_kernel,
        grid=(B // bb, L // tl),
        in_specs=[
            tok(COL_TILE), tok(SB_WIDTH), tok(D_MODEL), mod, mod, mod,
            pl.BlockSpec((2 * COL_TILE, D_MODEL), lambda b, i: (0, 0), pipeline_mode=pl.Buffered(1)),
            vec, vec,
        ],
        out_specs=[tok(D_MODEL), tok(D_MODEL)],
        out_shape=[jax.ShapeDtypeStruct((B, L, D_MODEL), F32), jax.ShapeDtypeStruct((B, L, D_MODEL), BF16)],
        compiler_params=_params("arbitrary", "arbitrary"),
        name="outproj",
    )(mix, att, x, gate, sc2, sh2, w_out, ln_g.reshape(1, 1, D_MODEL), ln_b.reshape(1, 1, D_MODEL))


def _ffn_kernel(h_ref, x_ref, g_ref, wup_ref, wdn_ref, lng_ref, lnb_ref, o_ref, acc_ref):
    f = pl.program_id(2)
    bb, tl, _ = x_ref.shape
    rows = bb * tl
    hid = jnp.dot(h_ref[...].reshape(rows, D_MODEL), wup_ref[...], preferred_element_type=F32)
    hid = jnp.square(jnp.maximum(hid, 0.0))
    part = jnp.dot(hid.astype(BF16), wdn_ref[...], preferred_element_type=F32)
    acc_ref[...] = jnp.where(f > 0, acc_ref[...], 0.0) + part

    @pl.when(f == pl.num_programs(2) - 1)
    def _():
        t = DEEPNORM_ALPHA * x_ref[...] + g_ref[...] * acc_ref[...].reshape(bb, tl, D_MODEL)
        o_ref[...] = _layer_norm(t) * lng_ref[...] + lnb_ref[...]


def _ffn(h2, x, gate, w_up, w_down, ln_g, ln_b, bb, tl):
    B, L, _ = x.shape
    tf = FF_TILE
    tok = pl.BlockSpec((bb, tl, D_MODEL), lambda b, i, f: (b, i, 0))
    mod = pl.BlockSpec((bb, 1, D_MODEL), lambda b, i, f: (b, 0, 0))
    vec = pl.BlockSpec((1, 1, D_MODEL), lambda b, i, f: (0, 0, 0))
    return pl.pallas_call(
        _ffn_kernel,
        grid=(B // bb, L // tl, D_FF // tf),
        in_specs=[
            tok, tok, mod,
            pl.BlockSpec((D_MODEL, tf), lambda b, i, f: (0, f)),
            pl.BlockSpec((tf, D_MODEL), lambda b, i, f: (f, 0)),
            vec, vec,
        ],
        out_specs=tok,
        out_shape=jax.ShapeDtypeStruct((B, L, D_MODEL), F32),
        scratch_shapes=[pltpu.VMEM((bb * tl, D_MODEL), F32)],
        compiler_params=_params("arbitrary", "arbitrary", "arbitrary"),
        name="ffn",
    )(h2, x, gate, w_up, w_down, ln_g.reshape(1, 1, D_MODEL), ln_b.reshape(1, 1, D_MODEL))


def _row_blocking(B, L):
    tl = min(L, ROW_TILE)
    bb = max(1, min(B, ROW_TILE // tl))
    return bb, tl


def _run_group(x, ada, hist, s5_re, s5_im, cache_k, cache_v, pos0, lw, s5_raw):
    B, L, _ = x.shape
    bb, tl = _row_blocking(B, L)
    chunk = min(L, S5_CHUNK)
    tab, _ = _s5prep(*s5_raw, chunk // SUBLANES)
    new_hist, new_re, new_im = [], [], []
    kv = None
    for l in range(DEPTH):
        w = lw[l]
        sh1, sc1, g1, sh2, sc2, g2 = [ada[l][:, None, n * D_MODEL:(n + 1) * D_MODEL] for n in range(6)]
        pu, k_all, v_all, proj_bf = _inproj(x, sc1, sh1, w["w_in"], kv, l, bb, tl)
        kv = (k_all, v_all)
        hist16 = jnp.pad(hist[l], ((0, 0), (HIST_ROWS - POOL_HIST, 0), (0, 0)))
        h0 = jnp.concatenate([s5_re[l].reshape(B, 1, S5_FLAT), s5_im[l].reshape(B, 1, S5_FLAT)], axis=-1)
        h0 = jnp.broadcast_to(h0, (B, SUBLANES, 2 * S5_FLAT))
        mix, nh, ns = _pools5(pu, hist16, h0, w["w_pool"], w["pool_scale"], tab[l:l + 1], w["b_blk"],
                              w["c_blk"], w["d_skip"], w["w_glu"], pos0, chunk)
        if cache_k is None:
            att = _attn_prompt(proj_bf)
        else:
            att = _attn_sample(proj_bf, cache_k, cache_v, l)
        x, h2 = _outproj(mix, att, x, g1, sc2, sh2, w["w_out"], w["ln1_g"], w["ln1_b"], bb, tl)
        x = _ffn(h2, x, g2, w["w_up"], w["w_down"], w["ln2_g"], w["ln2_b"], bb, tl)
        new_hist.append(nh[:, HIST_ROWS - POOL_HIST:, :])
        new_re.append(ns[:, 0, 0:S5_FLAT].reshape(B, S5_GROUPS, S5_STATE))
        new_im.append(ns[:, 0, S5_FLAT:].reshape(B, S5_GROUPS, S5_STATE))
    stack = lambda xs: jnp.stack(xs)
    return x, (stack(new_hist), stack(new_re), stack(new_im), kv[0], kv[1])


def kernel(x_prompt, x_sample, state_pool, state_s5_re, state_s5_im, cache_k, cache_v, c_prompt, c_sample,
           w_ada, b_ada, w_in, w_pool, pool_scale, s5_a_re, s5_a_im, s5_log_dt, s5_b_re, s5_b_im,
           s5_c_re, s5_c_im, s5_d, w_glu, w_out, ln1_g, ln1_b, w_up, w_down, ln2_g, ln2_b):
    B = x_prompt.shape[0]
    Bs = x_sample.shape[0]
    past = cache_k.shape[2]

    n_c = B + Bs
    pad = (-n_c) % 16
    c_all = jnp.concatenate([c_prompt, c_sample, jnp.zeros((pad, D_MODEL), F32)], axis=0)
    ada = _ada(c_all, w_ada, b_ada)
    ada_p, ada_s = ada[:, 0:B], ada[:, B:n_c]

    s5_raw = (s5_a_re, s5_a_im, s5_log_dt, s5_b_re, s5_b_im)
    _, bb_disc = _s5prep(*s5_raw, SUBLANES)
    lw = []
    for l in range(DEPTH):
        b_blk, c_blk = _s5_block_weights(bb_disc[l], s5_c_re[l], s5_c_im[l])
        lw.append(dict(
            w_in=w_in[l].astype(BF16), w_out=w_out[l].astype(BF16),
            w_up=w_up[l].astype(BF16), w_down=w_down[l].astype(BF16),
            w_glu=w_glu[l].astype(BF16), w_pool=w_pool[l].astype(BF16),
            pool_scale=pool_scale[l].reshape(1, POOL_WIDTH),
            d_skip=s5_d[l].reshape(1, S5_WIDTH),
            b_blk=b_blk, c_blk=c_blk,
            ln1_g=ln1_g[l], ln1_b=ln1_b[l], ln2_g=ln2_g[l], ln2_b=ln2_b[l],
        ))

    zero_hist = jnp.zeros((DEPTH, B, POOL_HIST, POOL_WIDTH), F32)
    zero_s5 = jnp.zeros((DEPTH, B, S5_GROUPS, S5_STATE), F32)
    y_p, (pool_p, re_p, im_p, k_p, v_p) = _run_group(
        x_prompt, ada_p, zero_hist, zero_s5, zero_s5, None, None, 0, lw, s5_raw)
    y_s, (pool_s, re_s, im_s, k_s, v_s) = _run_group(
        x_sample, ada_s, state_pool, state_s5_re, state_s5_im, cache_k, cache_v, past, lw, s5_raw)
    return (y_p, y_s, pool_p, re_p, im_p, k_p, v_p, pool_s, re_s, im_s, k_s, v_s)
```

```python
import functools
import math

import jax
import jax.numpy as jnp
from jax import lax
from jax.experimental import pallas as pl
from jax.experimental.pallas import tpu as pltpu

F32 = jnp.float32
BF16 = jnp.bfloat16

D_MODEL = 2048
DEPTH = 2
POOL_WIDTH = 512
POOL_WINDOWS = (2, 4, 8, 16)
POOL_GROUP = 128
POOL_HIST = 15
HIST_ROWS = 16
S5_WIDTH = 512
S5_GROUP_CH = 16
S5_GROUPS = 32
S5_STATE = 64
S5_FLAT = S5_GROUPS * S5_STATE
SB_WIDTH = 1024
SB_HEAD_DIM = 128
SB_HEADS = 8
D_FF = 4 * D_MODEL
IN_WIDTH = 4096
COL_TILE = 1024
DEEPNORM_ALPHA = (2 * DEPTH) ** 0.25
LN_EPS = 1e-5

V7X_VMEM_LIMIT = 56 * 1024 * 1024
SUBLANES = 8
ROW_TILE = 512
ROW_CHUNKS = 4
FF_TILE = 1024
CAST_TILE_BYTES = 4 * 1024 * 1024
S5_CHUNK = 256
S5_LANE_CHUNK = 512
S5_BLOCKS = S5_FLAT // S5_LANE_CHUNK
S5_BLOCK_CH = S5_WIDTH // S5_BLOCKS
S5_TABLES = 12
ATT_BLOCK = 128
STAGE_SKEW = 4
ATT_CHAINS = 16
LOG2_E = 1.4426950408889634
STICK_LOG2_FLOOR = -150.0
STICK_PARKED = -1e30


def _params(*sem):
    return pltpu.CompilerParams(dimension_semantics=sem, vmem_limit_bytes=V7X_VMEM_LIMIT)


def _layer_norm(x):
    mu = jnp.mean(x, axis=-1, keepdims=True)
    xc = x - mu
    var = jnp.mean(xc * xc, axis=-1, keepdims=True)
    return xc * lax.rsqrt(var + LN_EPS)


def _row_chunks(bb, tl):
    if bb >= ROW_CHUNKS:
        step = bb // ROW_CHUNKS
        return [(slice(c * step, (c + 1) * step), slice(0, tl)) for c in range(ROW_CHUNKS)]
    step = tl // ROW_CHUNKS
    return [(slice(0, bb), slice(c * step, (c + 1) * step)) for c in range(ROW_CHUNKS)]


def _cmul(a_re, a_im, b_re, b_im):
    return a_re * b_re - a_im * b_im, a_re * b_im + a_im * b_re


def _cast_kernel(w_ref, o_ref):
    o_ref[...] = w_ref[...].astype(BF16)


def _to_bf16(w):
    depth, k, n = w.shape
    rows = max(16, min(k, CAST_TILE_BYTES // (4 * n)))
    assert k % rows == 0
    spec = pl.BlockSpec((1, rows, n), lambda l, i: (l, i, 0))
    return pl.pallas_call(
        _cast_kernel,
        grid=(depth, k // rows),
        in_specs=[spec],
        out_specs=spec,
        out_shape=jax.ShapeDtypeStruct(w.shape, BF16),
        compiler_params=_params("arbitrary", "arbitrary"),
        name="to_bf16",
    )(w)


def _ada_kernel(c_ref, w_ref, b_ref, o_ref):
    c = c_ref[...]
    s = (c * jax.nn.sigmoid(c)).astype(BF16)
    w = w_ref[0].astype(BF16)
    o_ref[0] = jnp.dot(s, w, preferred_element_type=F32) + b_ref[0]


def _ada(c_all, w_ada, b_ada):
    rows = c_all.shape[0]
    n_out = w_ada.shape[-1]
    tn = 1024
    return pl.pallas_call(
        _ada_kernel,
        grid=(DEPTH, n_out // tn),
        in_specs=[
            pl.BlockSpec((rows, D_MODEL), lambda l, j: (0, 0)),
            pl.BlockSpec((1, D_MODEL, tn), lambda l, j: (l, 0, j)),
            pl.BlockSpec((1, 1, tn), lambda l, j: (l, 0, j)),
        ],
        out_specs=pl.BlockSpec((1, rows, tn), lambda l, j: (l, 0, j)),
        out_shape=jax.ShapeDtypeStruct((DEPTH, rows, n_out), F32),
        compiler_params=_params("arbitrary", "arbitrary"),
        name="ada",
    )(c_all, w_ada, b_ada.reshape(DEPTH, 1, n_out))


def _s5prep_kernel(seg, are_ref, aim_ref, ldt_ref, bre_ref, bim_ref, tab_ref, bb_ref):
    a_re = are_ref[0]
    a_im = aim_ref[0]
    dt = jnp.exp(ldt_ref[0])
    shape = (SUBLANES, S5_FLAT)
    row = lax.broadcasted_iota(jnp.int32, shape, 0)
    mag = jnp.broadcast_to(jnp.exp(a_re * dt), shape)
    ang = jnp.broadcast_to(a_im * dt, shape)
    ab_re = mag * jnp.cos(ang)
    ab_im = mag * jnp.sin(ang)
    tab_ref[0, 0] = ab_re
    tab_ref[0, 1] = ab_im
    p_re, p_im = ab_re, ab_im
    for _ in range(seg.bit_length() - 1):
        p_re, p_im = _cmul(p_re, p_im, p_re, p_im)
    ps_re = jnp.ones(shape, F32)
    ps_im = jnp.zeros(shape, F32)
    for n, k in enumerate((1, 2, 4)):
        tab_ref[0, 2 + 2 * n] = jnp.where(row >= k, p_re, 0.0)
        tab_ref[0, 3 + 2 * n] = jnp.where(row >= k, p_im, 0.0)
        q_re, q_im = _cmul(ps_re, ps_im, p_re, p_im)
        has_bit = (row & k) != 0
        ps_re = jnp.where(has_bit, q_re, ps_re)
        ps_im = jnp.where(has_bit, q_im, ps_im)
        p_re, p_im = _cmul(p_re, p_im, p_re, p_im)
    tab_ref[0, 8] = ps_re
    tab_ref[0, 9] = ps_im
    tab_ref[0, 10] = p_re
    tab_ref[0, 11] = p_im
    a1_re = ab_re[0:1, :]
    a1_im = ab_im[0:1, :]
    den = a_re * a_re + a_im * a_im
    f_re = ((a1_re - 1.0) * a_re + a1_im * a_im) / den
    f_im = (a1_im * a_re - (a1_re - 1.0) * a_im) / den
    b_re = bre_ref[0]
    b_im = bim_ref[0]
    bb_ref[0, 0] = f_re * b_re - f_im * b_im
    bb_ref[0, 1] = f_re * b_im + f_im * b_re


def _s5prep(a_re, a_im, log_dt, b_re, b_im, seg):
    assert seg & (seg - 1) == 0
    flat = lambda a: a.reshape(DEPTH, 1, S5_FLAT)
    ldt = jnp.broadcast_to(log_dt[:, :, None], (DEPTH, S5_GROUPS, S5_STATE)).reshape(DEPTH, 1, S5_FLAT)
    to_ch_major = lambda b: jnp.transpose(b, (0, 3, 1, 2)).reshape(DEPTH, S5_GROUP_CH, S5_FLAT)
    row_spec = pl.BlockSpec((1, 1, S5_FLAT), lambda l: (l, 0, 0))
    b_spec = pl.BlockSpec((1, S5_GROUP_CH, S5_FLAT), lambda l: (l, 0, 0))
    return pl.pallas_call(
        functools.partial(_s5prep_kernel, seg),
        grid=(DEPTH,),
        in_specs=[row_spec, row_spec, row_spec, b_spec, b_spec],
        out_specs=[
            pl.BlockSpec((1, S5_TABLES, SUBLANES, S5_FLAT), lambda l: (l, 0, 0, 0)),
            pl.BlockSpec((1, 2, S5_GROUP_CH, S5_FLAT), lambda l: (l, 0, 0, 0)),
        ],
        out_shape=[
            jax.ShapeDtypeStruct((DEPTH, S5_TABLES, SUBLANES, S5_FLAT), F32),
            jax.ShapeDtypeStruct((DEPTH, 2, S5_GROUP_CH, S5_FLAT), F32),
        ],
        compiler_params=_params("arbitrary"),
        name="s5prep",
    )(flat(a_re), flat(a_im), ldt, to_ch_major(b_re), to_ch_major(b_im))


def _s5_block_weights(bb, c_re, c_im):
    ch_group = jnp.arange(S5_WIDTH)[:, None] // S5_GROUP_CH
    st_group = jnp.arange(S5_FLAT)[None, :] // S5_STATE
    same = ch_group == st_group
    ch_b, st_b = S5_BLOCK_CH, S5_LANE_CHUNK

    def b_blocks(b):
        dense = jnp.where(same, jnp.tile(b, (S5_GROUPS, 1)), 0.0)
        return jnp.stack([dense[m * ch_b:(m + 1) * ch_b, m * st_b:(m + 1) * st_b] for m in range(S5_BLOCKS)])

    def c_blocks(c):
        c_sp = jnp.transpose(c, (0, 2, 1)).reshape(S5_FLAT, S5_GROUP_CH)
        dense = jnp.where(same.T, jnp.tile(c_sp, (1, S5_GROUPS)), 0.0)
        return jnp.stack([dense[m * st_b:(m + 1) * st_b, m * ch_b:(m + 1) * ch_b] for m in range(S5_BLOCKS)])

    b_blk = jnp.stack([b_blocks(bb[0]), b_blocks(bb[1])]).astype(BF16)
    c_blk = jnp.stack([c_blocks(c_re), c_blocks(c_im)]).astype(BF16)
    return b_blk, c_blk


def _inproj_kernel(x_ref, sc_ref, sh_ref, w_ref, *rest):
    pu_ref, k_ref, v_ref, bf_ref = rest[-4:]
    bb, tl, _ = x_ref.shape
    for bs, ts in _row_chunks(bb, tl):
        x = x_ref[bs, ts, :]
        cb, ct, _ = x.shape
        h = (_layer_norm(x) * (1.0 + sc_ref[bs]) + sh_ref[bs]).reshape(cb * ct, D_MODEL).astype(BF16)

        def cols(j, h=h, cb=cb, ct=ct):
            r = jnp.dot(h, w_ref[:, j * COL_TILE:(j + 1) * COL_TILE], preferred_element_type=F32)
            return r.reshape(cb, ct, COL_TILE)

        pu_ref[bs, ts, :] = cols(0)
        bf_ref[0, bs, ts, :] = cols(1).astype(BF16)
        k = cols(2)
        k_ref[0, bs, ts, :, :] = pltpu.einshape("bt(hd)->bthd", k, h=SB_HEADS)
        bf_ref[1, bs, ts, :] = k.astype(BF16)
        v = cols(3)
        v_ref[0, bs, ts, :, :] = pltpu.einshape("bt(hd)->bthd", v, h=SB_HEADS)
        bf_ref[2, bs, ts, :] = v.astype(BF16)


def _inproj(x, sc, sh, w_in, kv_prev, layer, bb, tl):
    B, L, _ = x.shape
    tok = lambda b, i: (b, i, 0)
    in_specs = [
        pl.BlockSpec((bb, tl, D_MODEL), tok),
        pl.BlockSpec((bb, 1, D_MODEL), lambda b, i: (b, 0, 0)),
        pl.BlockSpec((bb, 1, D_MODEL), lambda b, i: (b, 0, 0)),
        pl.BlockSpec((None, D_MODEL, IN_WIDTH), lambda b, i: (layer, 0, 0), pipeline_mode=pl.Buffered(1)),
    ]
    args = [x, sc, sh, w_in]
    aliases = {}
    if kv_prev is not None:
        in_specs += [pl.BlockSpec(memory_space=pl.ANY)] * 2
        args += list(kv_prev)
        aliases = {4: 1, 5: 2}
    kv_spec = pl.BlockSpec((1, bb, tl, SB_HEADS, SB_HEAD_DIM), lambda b, i: (layer, b, i, 0, 0))
    kv_shape = jax.ShapeDtypeStruct((DEPTH, B, L, SB_HEADS, SB_HEAD_DIM), F32)
    return pl.pallas_call(
        _inproj_kernel,
        grid=(B // bb, L // tl),
        in_specs=in_specs,
        out_specs=[
            pl.BlockSpec((bb, tl, COL_TILE), tok),
            kv_spec, kv_spec,
            pl.BlockSpec((3, bb, tl, COL_TILE), lambda b, i: (0, b, i, 0)),
        ],
        out_shape=[
            jax.ShapeDtypeStruct((B, L, COL_TILE), F32),
            kv_shape, kv_shape,
            jax.ShapeDtypeStruct((3, B, L, COL_TILE), BF16),
        ],
        input_output_aliases=aliases,
        compiler_params=_params("arbitrary", "arbitrary"),
        name="inproj",
    )(*args)


def _gelu_tanh(y):
    return 0.5 * y * (1.0 + jnp.tanh(math.sqrt(2.0 / math.pi) * (y + 0.044715 * (y * y * y))))


def _pools5_kernel(pos0, pu_ref, hist_ref, h0_ref, wpool_ref, pscale_ref, tab_ref, bblk_ref,
                   cblk_ref, dskip_ref, wglu_ref, mix_ref, nhist_ref, nstate_ref,
                   ext_ref, bu_ref, carry_ref):
    i = pl.program_id(1)
    T = pu_ref.shape[1]

    @pl.when(i == 0)
    def _():
        ext_ref[0:HIST_ROWS, :] = hist_ref[0]
        carry_ref[...] = h0_ref[0]

    p = pu_ref[0, :, 0:POOL_WIDTH]
    ext_ref[HIST_ROWS:HIST_ROWS + T, :] = p
    pos = pos0 + i * T + lax.broadcasted_iota(jnp.int32, (T, POOL_GROUP), 0)
    for g, w in enumerate(POOL_WINDOWS):
        lo, hi = g * POOL_GROUP, (g + 1) * POOL_GROUP
        win = ext_ref[HIST_ROWS:HIST_ROWS + T, lo:hi]
        for k in range(1, w):
            win = win + ext_ref[HIST_ROWS - k:HIST_ROWS - k + T, lo:hi]
        cnt = jnp.minimum(w, pos + 1).astype(F32)
        mixed = win / cnt - ext_ref[HIST_ROWS:HIST_ROWS + T, lo:hi]
        out = jnp.dot(mixed.astype(BF16), wpool_ref[g], preferred_element_type=F32)
        mix_ref[0, :, lo:hi] = (out * pscale_ref[:, lo:hi]).astype(BF16)
    last = ext_ref[T:T + HIST_ROWS, :]
    nhist_ref[0] = last
    ext_ref[0:HIST_ROWS, :] = last

    seg = T // SUBLANES
    shift = seg.bit_length() - 1
    n_idx = lax.broadcasted_iota(jnp.int32, (T, T), 0)
    t_idx = lax.broadcasted_iota(jnp.int32, (T, T), 1)
    perm = jnp.where(t_idx == (n_idx & (SUBLANES - 1)) * seg + (n_idx >> 3), 1.0, 0.0).astype(BF16)
    unperm = jnp.where(t_idx == (n_idx & (seg - 1)) * SUBLANES + (n_idx >> shift), 1.0, 0.0).astype(BF16)

    u = pu_ref[0, :, POOL_WIDTH:POOL_WIDTH + S5_WIDTH]
    u_hi = u.astype(BF16)
    u_lo = (u - u_hi.astype(F32)).astype(BF16)
    up_hi = jnp.dot(perm, u_hi, preferred_element_type=F32)
    u_perm = up_hi + jnp.dot(perm, u_lo, preferred_element_type=F32)
    ub = up_hi.astype(BF16)

    lc = S5_LANE_CHUNK
    row8 = lax.broadcasted_iota(jnp.int32, (SUBLANES, lc), 0)
    y_parts = []
    for m in range(S5_BLOCKS):
        re_l = slice(m * lc, (m + 1) * lc)
        im_l = slice(S5_FLAT + m * lc, S5_FLAT + (m + 1) * lc)
        ch_l = slice(m * S5_BLOCK_CH, (m + 1) * S5_BLOCK_CH)
        bu_ref[:, re_l] = jnp.dot(ub[:, ch_l], bblk_ref[0, m], preferred_element_type=F32)
        bu_ref[:, im_l] = jnp.dot(ub[:, ch_l], bblk_ref[1, m], preferred_element_type=F32)
        a_re = tab_ref[0, 0, :, re_l]
        a_im = tab_ref[0, 1, :, re_l]

        f_re = f_im = jnp.zeros((SUBLANES, lc), F32)
        for i in range(seg):
            rows = slice(i * SUBLANES, (i + 1) * SUBLANES)
            g_re, g_im = _cmul(a_re, a_im, f_re, f_im)
            f_re = g_re + bu_ref[rows, re_l]
            f_im = g_im + bu_ref[rows, im_l]
            bu_ref[rows, re_l] = f_re
            bu_ref[rows, im_l] = f_im

        for n, k in enumerate((1, 2, 4)):
            d_re, d_im = _cmul(tab_ref[0, 2 + 2 * n, :, re_l], tab_ref[0, 3 + 2 * n, :, re_l],
                               pltpu.roll(f_re, k, 0), pltpu.roll(f_im, k, 0))
            f_re, f_im = f_re + d_re, f_im + d_im
        c_re = carry_ref[:, re_l]
        c_im = carry_ref[:, im_l]
        e_re, e_im = _cmul(tab_ref[0, 8, :, re_l], tab_ref[0, 9, :, re_l], c_re, c_im)
        e_re = e_re + jnp.where(row8 >= 1, pltpu.roll(f_re, 1, 0), 0.0)
        e_im = e_im + jnp.where(row8 >= 1, pltpu.roll(f_im, 1, 0), 0.0)
        n_re, n_im = _cmul(tab_ref[0, 10, :, re_l], tab_ref[0, 11, :, re_l], c_re, c_im)
        carry_ref[:, re_l] = n_re + jnp.broadcast_to(f_re[SUBLANES - 1:SUBLANES, :], (SUBLANES, lc))
        carry_ref[:, im_l] = n_im + jnp.broadcast_to(f_im[SUBLANES - 1:SUBLANES, :], (SUBLANES, lc))

        w_re, w_im = _cmul(a_re, a_im, e_re, e_im)
        for i in range(seg):
            rows = slice(i * SUBLANES, (i + 1) * SUBLANES)
            bu_ref[rows, re_l] = bu_ref[rows, re_l] + w_re
            bu_ref[rows, im_l] = bu_ref[rows, im_l] + w_im
            if i + 1 < seg:
                w_re, w_im = _cmul(a_re, a_im, w_re, w_im)

        y_parts.append(
            jnp.dot(bu_ref[:, re_l].astype(BF16), cblk_ref[0, m], preferred_element_type=F32)
            - jnp.dot(bu_ref[:, im_l].astype(BF16), cblk_ref[1, m], preferred_element_type=F32))

    nstate_ref[0] = carry_ref[0:1, :]
    y = _gelu_tanh(jnp.concatenate(y_parts, axis=1) + dskip_ref[...] * u_perm)
    gate = jnp.dot(y.astype(BF16), wglu_ref[...], preferred_element_type=F32)
    s5_perm = (y * jax.nn.sigmoid(gate)).astype(BF16)
    mix_ref[0, :, POOL_WIDTH:POOL_WIDTH + S5_WIDTH] = jnp.dot(
        unperm, s5_perm, preferred_element_type=F32).astype(BF16)


def _pools5(pu, hist, h0, w_pool, pool_scale, tab, b_blk, c_blk, d_skip, w_glu, pos0, chunk):
    B, L, _ = pu.shape
    T = chunk
    const2 = lambda b, i: (0, 0)
    const4 = lambda b, i: (0, 0, 0, 0)
    return pl.pallas_call(
        functools.partial(_pools5_kernel, pos0),
        grid=(B, L // T),
        in_specs=[
            pl.BlockSpec((1, T, COL_TILE), lambda b, i: (b, i, 0)),
            pl.BlockSpec((1, HIST_ROWS, POOL_WIDTH), lambda b, i: (b, 0, 0)),
            pl.BlockSpec((1, SUBLANES, 2 * S5_FLAT), lambda b, i: (b, 0, 0)),
            pl.BlockSpec((len(POOL_WINDOWS), POOL_GROUP, POOL_GROUP), lambda b, i: (0, 0, 0)),
            pl.BlockSpec((1, POOL_WIDTH), const2),
            pl.BlockSpec((1, S5_TABLES, SUBLANES, S5_FLAT), const4),
            pl.BlockSpec((2, S5_BLOCKS, S5_BLOCK_CH, S5_LANE_CHUNK), const4),
            pl.BlockSpec((2, S5_BLOCKS, S5_LANE_CHUNK, S5_BLOCK_CH), const4),
            pl.BlockSpec((1, S5_WIDTH), const2),
            pl.BlockSpec((S5_WIDTH, S5_WIDTH), const2),
        ],
        out_specs=[
            pl.BlockSpec((1, T, COL_TILE), lambda b, i: (b, i, 0)),
            pl.BlockSpec((1, HIST_ROWS, POOL_WIDTH), lambda b, i: (b, 0, 0)),
            pl.BlockSpec((1, 1, 2 * S5_FLAT), lambda b, i: (b, 0, 0)),
        ],
        out_shape=[
            jax.ShapeDtypeStruct((B, L, COL_TILE), BF16),
            jax.ShapeDtypeStruct((B, HIST_ROWS, POOL_WIDTH), F32),
            jax.ShapeDtypeStruct((B, 1, 2 * S5_FLAT), F32),
        ],
        scratch_shapes=[
            pltpu.VMEM((HIST_ROWS + T, POOL_WIDTH), F32),
            pltpu.VMEM((T, 2 * S5_FLAT), F32),
            pltpu.VMEM((SUBLANES, 2 * S5_FLAT), F32),
        ],
        compiler_params=_params("arbitrary", "arbitrary"),
        name="pools5",
    )(pu, hist, h0, w_pool, pool_scale, tab, b_blk, c_blk, d_skip, w_glu)


def _tri_ones(n):
    j = lax.broadcasted_iota(jnp.int32, (2 * n, 2 * n), 0)
    j = jnp.where(j >= n, j - n, j)
    s = lax.broadcasted_iota(jnp.int32, (2 * n, 2 * n), 1)
    return jnp.where((j >= s) | (s >= n), 1.0, 0.0).astype(BF16)


def _scores(q, k):
    return lax.dot_general(q, k, (((1,), (1,)), ((), ())),
                           preferred_element_type=F32) * (SB_HEAD_DIM ** -0.5 * LOG2_E)


def _stick_blocks(scores, vs, mask, laters, tri, stored_scores=False, after=None):
    n = vs[0].shape[0]
    n_pairs = len(vs)
    zs, sums, results = {}, {}, []
    for step in range(n_pairs + 2 * STAGE_SKEW):
        a, b, c = step, step - STAGE_SKEW, step - 2 * STAGE_SKEW
        if a < n_pairs and not stored_scores:
            zs[a] = scores[a]()
        if 0 <= b < n_pairs:
            z = scores[b]() if stored_scores else zs[b]
            sp = jnp.maximum(z, 0.0) + jnp.log2(1.0 + jnp.exp2(-jnp.abs(z)))
            if mask is not None:
                sp = jnp.where(mask, sp, 0.0)
            sp_hi = sp.astype(BF16)
            sp_lo = (sp - sp_hi.astype(F32)).astype(BF16)
            sums[b] = jnp.dot(jnp.concatenate([sp_hi, sp_lo], axis=1), tri, preferred_element_type=F32)
        if 0 <= c < n_pairs:
            s, later = sums.pop(c), laters[c]
            z = scores[c]() if stored_scores else zs.pop(c)
            w = jnp.exp2(z + (later - s[:, 0:n]))
            if mask is not None:
                w = jnp.where(mask, w, 0.0)
            results.append((jnp.dot(w.astype(BF16), vs[c], preferred_element_type=F32), later - s[:, n:2 * n]))
            if after is not None:
                after[c]()
    return results


def _attn_prompt_kernel(q_ref, k_ref, v_ref, o_ref, acc_ref, later_ref, z_ref):
    tile = pl.program_id(2)
    blk = ATT_BLOCK
    n_chain = q_ref.shape[2] // blk
    chains = range(n_chain)
    tri = _tri_ones(blk)
    row = lax.broadcasted_iota(jnp.int32, (blk, blk), 0)
    col = lax.broadcasted_iota(jnp.int32, (blk, blk), 1)
    causal = col < row
    zeros = jnp.zeros((blk, blk), F32)

    def key_rows(kb):
        return pl.ds(pl.multiple_of(jnp.maximum(kb, 0) * blk, blk), blk)

    def q_of(c):
        return q_ref[0, 0, c * blk:(c + 1) * blk, :]

    def score_ahead(c, kb):
        def run():
            z_ref[c] = _scores(q_of(c), k_ref[0, 0, key_rows(kb), :])
        return run

    def stored(c):
        return lambda: z_ref[c]

    diag = [tile * n_chain + c for c in chains]
    results = _stick_blocks([lambda c=c: _scores(q_of(c), k_ref[0, 0, key_rows(diag[c]), :]) for c in chains],
                            [v_ref[0, 0, key_rows(diag[c]), :] for c in chains], causal,
                            [zeros] * n_chain, tri, after=[score_ahead(c, diag[c] - 1) for c in chains])
    alive = None
    for c, (out, later) in enumerate(results):
        acc_ref[c] = out
        later_ref[c] = later
        alive = later if alive is None else jnp.maximum(alive, later)

    def cond(carry):
        t, go = carry
        return jnp.logical_and(t <= tile * n_chain + (n_chain - 1), go > 0)

    def body(carry):
        t, _ = carry
        kbs = [diag[c] - t for c in chains]
        laters = [jnp.where(kbs[c] < 0, STICK_PARKED, later_ref[c]) for c in chains]
        results = _stick_blocks([stored(c) for c in chains], [v_ref[0, 0, key_rows(kbs[c]), :] for c in chains],
                                None, laters, tri, stored_scores=True,
                                after=[score_ahead(c, kbs[c] - 1) for c in chains])
        alive = None
        for c, (out, later) in enumerate(results):
            acc_ref[c] += out
            later_ref[c] = later
            alive = later if alive is None else jnp.maximum(alive, later)
        return t + 1, (jnp.max(alive) >= STICK_LOG2_FLOOR).astype(jnp.int32)

    go = (jnp.max(alive) >= STICK_LOG2_FLOOR).astype(jnp.int32)
    lax.while_loop(cond, body, (jnp.int32(1), go))
    for c in chains:
        o_ref[0, c * blk:(c + 1) * blk, :] = acc_ref[c].astype(BF16)


def _attn_prompt(proj_bf):
    _, B, L, _ = proj_bf.shape
    tq = min(ATT_CHAINS * ATT_BLOCK, L)
    kv_spec = lambda part: pl.BlockSpec((1, 1, L, SB_HEAD_DIM), lambda b, h, i: (part, b, 0, h))
    return pl.pallas_call(
        _attn_prompt_kernel,
        grid=(B, SB_HEADS, L // tq),
        in_specs=[
            pl.BlockSpec((1, 1, tq, SB_HEAD_DIM), lambda b, h, i: (0, b, i, h)),
            kv_spec(1),
            kv_spec(2),
        ],
        out_specs=pl.BlockSpec((1, tq, SB_HEAD_DIM), lambda b, h, i: (b, i, h)),
        out_shape=jax.ShapeDtypeStruct((B, L, SB_WIDTH), BF16),
        scratch_shapes=[
            pltpu.VMEM((tq // ATT_BLOCK, ATT_BLOCK, ATT_BLOCK), F32),
            pltpu.VMEM((tq // ATT_BLOCK, ATT_BLOCK, ATT_BLOCK), F32),
            pltpu.VMEM((tq // ATT_BLOCK, ATT_BLOCK, ATT_BLOCK), F32),
        ],
        compiler_params=_params("arbitrary", "arbitrary", "arbitrary"),
        name="attn_prompt",
    )(proj_bf, proj_bf, proj_bf)


def _attn_sample_kernel(q_ref, kn_ref, vn_ref, kc_ref, vc_ref, o_ref):
    L = q_ref.shape[2]
    past = kc_ref.shape[2]
    cblk = min(ATT_BLOCK, past)
    tri_new = _tri_ones(L)
    tri_c = _tri_ones(cblk)
    row = lax.broadcasted_iota(jnp.int32, (L, L), 0)
    col = lax.broadcasted_iota(jnp.int32, (L, L), 1)
    causal = col < row
    lanes = [slice(h * SB_HEAD_DIM, (h + 1) * SB_HEAD_DIM) for h in range(SB_HEADS)]
    qs = [q_ref[0, 0, :, ln] for ln in lanes]
    results = _stick_blocks([lambda q=q, ln=ln: _scores(q, kn_ref[0, 0, :, ln]) for q, ln in zip(qs, lanes)],
                            [vn_ref[0, 0, :, ln] for ln in lanes], causal,
                            [jnp.zeros((L, L), F32)] * SB_HEADS, tri_new)
    accs = [out for out, _ in results]
    laters = [jnp.broadcast_to(later[:, 0:1], (L, cblk)) for _, later in results]
    for j in range(past // cblk - 1, -1, -1):
        rows = slice(j * cblk, (j + 1) * cblk)
        kc = pltpu.einshape("phd->p(hd)", kc_ref[0, 0, rows, :, :]).astype(BF16)
        vc = pltpu.einshape("phd->p(hd)", vc_ref[0, 0, rows, :, :]).astype(BF16)
        results = _stick_blocks([lambda q=q, ln=ln, kc=kc: _scores(q, kc[:, ln]) for q, ln in zip(qs, lanes)],
                                [vc[:, ln] for ln in lanes], None, laters, tri_c)
        accs = [acc + out for acc, (out, _) in zip(accs, results)]
        laters = [later for _, later in results]
    for ln, acc in zip(lanes, accs):
        o_ref[0, :, ln] = acc.astype(BF16)


def _attn_sample(proj_bf, cache_k, cache_v, layer):
    _, B, L, _ = proj_bf.shape
    past = cache_k.shape[2]
    new_spec = lambda part: pl.BlockSpec((1, 1, L, SB_WIDTH), lambda b: (part, b, 0, 0))
    cache_spec = pl.BlockSpec((1, 1, past, SB_HEADS, SB_HEAD_DIM), lambda b: (layer, b, 0, 0, 0))
    return pl.pallas_call(
        _attn_sample_kernel,
        grid=(B,),
        in_specs=[new_spec(0), new_spec(1), new_spec(2), cache_spec, cache_spec],
        out_specs=pl.BlockSpec((1, L, SB_WIDTH), lambda b: (b, 0, 0)),
        out_shape=jax.ShapeDtypeStruct((B, L, SB_WIDTH), BF16),
        compiler_params=_params("arbitrary"),
        name="attn_sample",
    )(proj_bf, proj_bf, proj_bf, cache_k, cache_v)


def _outproj_kernel(mix_ref, att_ref, x_ref, g_ref, sc_ref, sh_ref, w_ref, lng_ref, lnb_ref, o_ref, h_ref):
    bb, tl, _ = x_ref.shape
    half = mix_ref.shape[-1]
    for bs, ts in _row_chunks(bb, tl):
        m = mix_ref[bs, ts, :]
        cb, ct, _ = m.shape
        mix = (jnp.dot(m.reshape(cb * ct, half), w_ref[0:half, :], preferred_element_type=F32)
               + jnp.dot(att_ref[bs, ts, :].reshape(cb * ct, half), w_ref[half:2 * half, :],
                         preferred_element_type=F32))
        t = DEEPNORM_ALPHA * x_ref[bs, ts, :] + g_ref[bs] * mix.reshape(cb, ct, D_MODEL)
        x1 = _layer_norm(t) * lng_ref[...] + lnb_ref[...]
        o_ref[bs, ts, :] = x1
        h_ref[bs, ts, :] = (_layer_norm(x1) * (1.0 + sc_ref[bs]) + sh_ref[bs]).astype(BF16)


def _outproj(mix, att, x, gate, sc2, sh2, w_out, layer, ln_g, ln_b, bb, tl):
    B, L, _ = x.shape
    tok = lambda width: pl.BlockSpec((bb, tl, width), lambda b, i: (b, i, 0))
    mod = pl.BlockSpec((bb, 1, D_MODEL), lambda b, i: (b, 0, 0))
    vec = pl.BlockSpec((1, 1, D_MODEL), lambda b, i: (0, 0, 0))
    return pl.pallas_call(
        _outproj_kernel,
        grid=(B // bb, L // tl),
        in_specs=[
            tok(COL_TILE), tok(SB_WIDTH), tok(D_MODEL), mod, mod, mod,
            pl.BlockSpec((None, 2 * COL_TILE, D_MODEL), lambda b, i: (layer, 0, 0),
                         pipeline_mode=pl.Buffered(1)),
            vec, vec,
        ],
        out_specs=[tok(D_MODEL), tok(D_MODEL)],
        out_shape=[jax.ShapeDtypeStruct((B, L, D_MODEL), F32), jax.ShapeDtypeStruct((B, L, D_MODEL), BF16)],
        compiler_params=_params("arbitrary", "arbitrary"),
        name="outproj",
    )(mix, att, x, gate, sc2, sh2, w_out, ln_g.reshape(1, 1, D_MODEL), ln_b.reshape(1, 1, D_MODEL))


def _ffn_kernel(h_ref, x_ref, g_ref, wup_ref, wdn_ref, lng_ref, lnb_ref, o_ref, acc_ref):
    f = pl.program_id(2)
    bb, tl, _ = x_ref.shape
    rows = bb * tl

    def partial_sum():
        hid = jnp.dot(h_ref[...].reshape(rows, D_MODEL), wup_ref[...], preferred_element_type=F32)
        hid = jnp.square(jnp.maximum(hid, 0.0))
        return jnp.dot(hid.astype(BF16), wdn_ref[...], preferred_element_type=F32)

    @pl.when(f == 0)
    def _():
        acc_ref[...] = partial_sum()

    @pl.when(f > 0)
    def _():
        acc_ref[...] += partial_sum()

    @pl.when(f == pl.num_programs(2) - 1)
    def _():
        t = DEEPNORM_ALPHA * x_ref[...] + g_ref[...] * acc_ref[...].reshape(bb, tl, D_MODEL)
        o_ref[...] = _layer_norm(t) * lng_ref[...] + lnb_ref[...]


def _ffn(h2, x, gate, w_up, w_down, layer, ln_g, ln_b, bb, tl):
    B, L, _ = x.shape
    tf = FF_TILE
    tok = pl.BlockSpec((bb, tl, D_MODEL), lambda b, i, f: (b, i, 0))
    mod = pl.BlockSpec((bb, 1, D_MODEL), lambda b, i, f: (b, 0, 0))
    vec = pl.BlockSpec((1, 1, D_MODEL), lambda b, i, f: (0, 0, 0))
    return pl.pallas_call(
        _ffn_kernel,
        grid=(B // bb, L // tl, D_FF // tf),
        in_specs=[
            tok, tok, mod,
            pl.BlockSpec((None, D_MODEL, tf), lambda b, i, f: (layer, 0, f)),
            pl.BlockSpec((None, tf, D_MODEL), lambda b, i, f: (layer, f, 0)),
            vec, vec,
        ],
        out_specs=tok,
        out_shape=jax.ShapeDtypeStruct((B, L, D_MODEL), F32),
        scratch_shapes=[pltpu.VMEM((bb * tl, D_MODEL), F32)],
        compiler_params=_params("arbitrary", "arbitrary", "arbitrary"),
        name="ffn",
    )(h2, x, gate, w_up, w_down, ln_g.reshape(1, 1, D_MODEL), ln_b.reshape(1, 1, D_MODEL))


def _row_blocking(B, L):
    tl = min(L, ROW_TILE)
    bb = max(1, min(B, ROW_TILE // tl))
    return bb, tl


def _run_group(x, ada, hist, s5_re, s5_im, cache_k, cache_v, pos0, lw, s5_raw):
    B, L, _ = x.shape
    bb, tl = _row_blocking(B, L)
    chunk = min(L, S5_CHUNK)
    tab, _ = _s5prep(*s5_raw, chunk // SUBLANES)
    new_hist, new_re, new_im = [], [], []
    kv = None
    for l in range(DEPTH):
        w = lw[l]
        sh1, sc1, g1, sh2, sc2, g2 = [ada[l][:, None, n * D_MODEL:(n + 1) * D_MODEL] for n in range(6)]
        pu, k_all, v_all, proj_bf = _inproj(x, sc1, sh1, w["w_in"], kv, l, bb, tl)
        kv = (k_all, v_all)
        hist16 = jnp.pad(hist[l], ((0, 0), (HIST_ROWS - POOL_HIST, 0), (0, 0)))
        h0 = jnp.concatenate([s5_re[l].reshape(B, 1, S5_FLAT), s5_im[l].reshape(B, 1, S5_FLAT)], axis=-1)
        h0 = jnp.broadcast_to(h0, (B, SUBLANES, 2 * S5_FLAT))
        mix, nh, ns = _pools5(pu, hist16, h0, w["w_pool"], w["pool_scale"], tab[l:l + 1], w["b_blk"],
                              w["c_blk"], w["d_skip"], w["w_glu"], pos0, chunk)
        if cache_k is None:
            att = _attn_prompt(proj_bf)
        else:
            att = _attn_sample(proj_bf, cache_k, cache_v, l)
        x, h2 = _outproj(mix, att, x, g1, sc2, sh2, w["w_out"], l, w["ln1_g"], w["ln1_b"], bb, tl)
        x = _ffn(h2, x, g2, w["w_up"], w["w_down"], l, w["ln2_g"], w["ln2_b"], bb, tl)
        new_hist.append(nh[:, HIST_ROWS - POOL_HIST:, :])
        new_re.append(ns[:, 0, 0:S5_FLAT].reshape(B, S5_GROUPS, S5_STATE))
        new_im.append(ns[:, 0, S5_FLAT:].reshape(B, S5_GROUPS, S5_STATE))
    stack = lambda xs: jnp.stack(xs)
    return x, (stack(new_hist), stack(new_re), stack(new_im), kv[0], kv[1])


def kernel(x_prompt, x_sample, state_pool, state_s5_re, state_s5_im, cache_k, cache_v, c_prompt, c_sample,
           w_ada, b_ada, w_in, w_pool, pool_scale, s5_a_re, s5_a_im, s5_log_dt, s5_b_re, s5_b_im,
           s5_c_re, s5_c_im, s5_d, w_glu, w_out, ln1_g, ln1_b, w_up, w_down, ln2_g, ln2_b):
    B = x_prompt.shape[0]
    Bs = x_sample.shape[0]
    past = cache_k.shape[2]

    n_c = B + Bs
    pad = (-n_c) % 16
    c_all = jnp.concatenate([c_prompt, c_sample, jnp.zeros((pad, D_MODEL), F32)], axis=0)
    ada = _ada(c_all, w_ada, b_ada)
    ada_p, ada_s = ada[:, 0:B], ada[:, B:n_c]

    s5_raw = (s5_a_re, s5_a_im, s5_log_dt, s5_b_re, s5_b_im)
    _, bb_disc = _s5prep(*s5_raw, SUBLANES)
    w_in_bf, w_out_bf, w_up_bf, w_down_bf = [_to_bf16(w) for w in (w_in, w_out, w_up, w_down)]
    lw = []
    for l in range(DEPTH):
        b_blk, c_blk = _s5_block_weights(bb_disc[l], s5_c_re[l], s5_c_im[l])
        lw.append(dict(
            w_in=w_in_bf, w_out=w_out_bf, w_up=w_up_bf, w_down=w_down_bf,
            w_glu=w_glu[l].astype(BF16), w_pool=w_pool[l].astype(BF16),
            pool_scale=pool_scale[l].reshape(1, POOL_WIDTH),
            d_skip=s5_d[l].reshape(1, S5_WIDTH),
            b_blk=b_blk, c_blk=c_blk,
            ln1_g=ln1_g[l], ln1_b=ln1_b[l], ln2_g=ln2_g[l], ln2_b=ln2_b[l],
        ))

    zero_hist = jnp.zeros((DEPTH, B, POOL_HIST, POOL_WIDTH), F32)
    zero_s5 = jnp.zeros((DEPTH, B, S5_GROUPS, S5_STATE), F32)
    y_p, (pool_p, re_p, im_p, k_p, v_p) = _run_group(
        x_prompt, ada_p, zero_hist, zero_s5, zero_s5, None, None, 0, lw, s5_raw)
    y_s, (pool_s, re_s, im_s, k_s, v_s) = _run_group(
        x_sample, ada_s, state_pool, state_s5_re, state_s5_im, cache_k, cache_v, past, lw, s5_raw)
    return (y_p, y_s, pool_p, re_p, im_p, k_p, v_p, pool_s, re_s, im_s, k_s, v_s)
```

```python
import functools
import math

import jax
import jax.numpy as jnp
from jax import lax
from jax.experimental import pallas as pl
from jax.experimental.pallas import tpu as pltpu

F32 = jnp.float32
BF16 = jnp.bfloat16

D_MODEL = 2048
DEPTH = 2
POOL_WIDTH = 512
POOL_WINDOWS = (2, 4, 8, 16)
POOL_GROUP = 128
POOL_HIST = 15
HIST_ROWS = 16
S5_WIDTH = 512
S5_GROUP_CH = 16
S5_GROUPS = 32
S5_STATE = 64
S5_FLAT = S5_GROUPS * S5_STATE
SB_WIDTH = 1024
SB_HEAD_DIM = 128
SB_HEADS = 8
D_FF = 4 * D_MODEL
IN_WIDTH = 4096
COL_TILE = 1024
DEEPNORM_ALPHA = (2 * DEPTH) ** 0.25
LN_EPS = 1e-5

V7X_VMEM_LIMIT = 56 * 1024 * 1024
SUBLANES = 8
ROW_TILE = 512
ROW_CHUNKS = 4
FF_TILE = 1024
CAST_TILE_BYTES = 4 * 1024 * 1024
S5_CHUNK = 512
S5_LANE_CHUNK = 512
S5_BLOCKS = S5_FLAT // S5_LANE_CHUNK
S5_BLOCK_CH = S5_WIDTH // S5_BLOCKS
S5_TABLES = 12
ATT_BLOCK = 128
STAGE_SKEW = 4
ATT_CHAINS = 32
LOG2_E = 1.4426950408889634
STICK_LOG2_FLOOR = -150.0
STICK_PARKED = -1e30


def _params(*sem):
    return pltpu.CompilerParams(dimension_semantics=sem, vmem_limit_bytes=V7X_VMEM_LIMIT)


def _layer_norm(x):
    mu = jnp.mean(x, axis=-1, keepdims=True)
    xc = x - mu
    var = jnp.mean(xc * xc, axis=-1, keepdims=True)
    return xc * lax.rsqrt(var + LN_EPS)


def _row_chunks(bb, tl):
    if bb >= ROW_CHUNKS:
        step = bb // ROW_CHUNKS
        return [(slice(c * step, (c + 1) * step), slice(0, tl)) for c in range(ROW_CHUNKS)]
    step = tl // ROW_CHUNKS
    return [(slice(0, bb), slice(c * step, (c + 1) * step)) for c in range(ROW_CHUNKS)]


def _cmul(a_re, a_im, b_re, b_im):
    return a_re * b_re - a_im * b_im, a_re * b_im + a_im * b_re


def _cast_kernel(w_ref, o_ref):
    o_ref[...] = w_ref[...].astype(BF16)


def _to_bf16(w):
    depth, k, n = w.shape
    rows = max(16, min(k, CAST_TILE_BYTES // (4 * n)))
    assert k % rows == 0
    spec = pl.BlockSpec((1, rows, n), lambda l, i: (l, i, 0))
    return pl.pallas_call(
        _cast_kernel,
        grid=(depth, k // rows),
        in_specs=[spec],
        out_specs=spec,
        out_shape=jax.ShapeDtypeStruct(w.shape, BF16),
        compiler_params=_params("arbitrary", "arbitrary"),
        name="to_bf16",
    )(w)


def _ada_kernel(c_ref, w_ref, b_ref, o_ref):
    c = c_ref[...]
    s = (c * jax.nn.sigmoid(c)).astype(BF16)
    w = w_ref[0].astype(BF16)
    o_ref[0] = jnp.dot(s, w, preferred_element_type=F32) + b_ref[0]


def _ada(c_all, w_ada, b_ada):
    rows = c_all.shape[0]
    n_out = w_ada.shape[-1]
    tn = 1024
    return pl.pallas_call(
        _ada_kernel,
        grid=(DEPTH, n_out // tn),
        in_specs=[
            pl.BlockSpec((rows, D_MODEL), lambda l, j: (0, 0)),
            pl.BlockSpec((1, D_MODEL, tn), lambda l, j: (l, 0, j)),
            pl.BlockSpec((1, 1, tn), lambda l, j: (l, 0, j)),
        ],
        out_specs=pl.BlockSpec((1, rows, tn), lambda l, j: (l, 0, j)),
        out_shape=jax.ShapeDtypeStruct((DEPTH, rows, n_out), F32),
        compiler_params=_params("arbitrary", "arbitrary"),
        name="ada",
    )(c_all, w_ada, b_ada.reshape(DEPTH, 1, n_out))


def _s5prep_kernel(seg, are_ref, aim_ref, ldt_ref, bre_ref, bim_ref, tab_ref, bb_ref):
    a_re = are_ref[0]
    a_im = aim_ref[0]
    dt = jnp.exp(ldt_ref[0])
    shape = (SUBLANES, S5_FLAT)
    row = lax.broadcasted_iota(jnp.int32, shape, 0)
    mag = jnp.broadcast_to(jnp.exp(a_re * dt), shape)
    ang = jnp.broadcast_to(a_im * dt, shape)
    ab_re = mag * jnp.cos(ang)
    ab_im = mag * jnp.sin(ang)
    tab_ref[0, 0] = ab_re
    tab_ref[0, 1] = ab_im
    p_re, p_im = ab_re, ab_im
    for _ in range(seg.bit_length() - 1):
        p_re, p_im = _cmul(p_re, p_im, p_re, p_im)
    ps_re = jnp.ones(shape, F32)
    ps_im = jnp.zeros(shape, F32)
    for n, k in enumerate((1, 2, 4)):
        tab_ref[0, 2 + 2 * n] = jnp.where(row >= k, p_re, 0.0)
        tab_ref[0, 3 + 2 * n] = jnp.where(row >= k, p_im, 0.0)
        q_re, q_im = _cmul(ps_re, ps_im, p_re, p_im)
        has_bit = (row & k) != 0
        ps_re = jnp.where(has_bit, q_re, ps_re)
        ps_im = jnp.where(has_bit, q_im, ps_im)
        p_re, p_im = _cmul(p_re, p_im, p_re, p_im)
    tab_ref[0, 8] = ps_re
    tab_ref[0, 9] = ps_im
    tab_ref[0, 10] = p_re
    tab_ref[0, 11] = p_im
    a1_re = ab_re[0:1, :]
    a1_im = ab_im[0:1, :]
    den = a_re * a_re + a_im * a_im
    f_re = ((a1_re - 1.0) * a_re + a1_im * a_im) / den
    f_im = (a1_im * a_re - (a1_re - 1.0) * a_im) / den
    b_re = bre_ref[0]
    b_im = bim_ref[0]
    bb_ref[0, 0] = f_re * b_re - f_im * b_im
    bb_ref[0, 1] = f_re * b_im + f_im * b_re


def _s5prep(a_re, a_im, log_dt, b_re, b_im, seg):
    assert seg & (seg - 1) == 0
    flat = lambda a: a.reshape(DEPTH, 1, S5_FLAT)
    ldt = jnp.broadcast_to(log_dt[:, :, None], (DEPTH, S5_GROUPS, S5_STATE)).reshape(DEPTH, 1, S5_FLAT)
    to_ch_major = lambda b: jnp.transpose(b, (0, 3, 1, 2)).reshape(DEPTH, S5_GROUP_CH, S5_FLAT)
    row_spec = pl.BlockSpec((1, 1, S5_FLAT), lambda l: (l, 0, 0))
    b_spec = pl.BlockSpec((1, S5_GROUP_CH, S5_FLAT), lambda l: (l, 0, 0))
    return pl.pallas_call(
        functools.partial(_s5prep_kernel, seg),
        grid=(DEPTH,),
        in_specs=[row_spec, row_spec, row_spec, b_spec, b_spec],
        out_specs=[
            pl.BlockSpec((1, S5_TABLES, SUBLANES, S5_FLAT), lambda l: (l, 0, 0, 0)),
            pl.BlockSpec((1, 2, S5_GROUP_CH, S5_FLAT), lambda l: (l, 0, 0, 0)),
        ],
        out_shape=[
            jax.ShapeDtypeStruct((DEPTH, S5_TABLES, SUBLANES, S5_FLAT), F32),
            jax.ShapeDtypeStruct((DEPTH, 2, S5_GROUP_CH, S5_FLAT), F32),
        ],
        compiler_params=_params("arbitrary"),
        name="s5prep",
    )(flat(a_re), flat(a_im), ldt, to_ch_major(b_re), to_ch_major(b_im))


def _s5_block_weights(bb, c_re, c_im):
    ch_group = jnp.arange(S5_WIDTH)[:, None] // S5_GROUP_CH
    st_group = jnp.arange(S5_FLAT)[None, :] // S5_STATE
    same = ch_group == st_group
    ch_b, st_b = S5_BLOCK_CH, S5_LANE_CHUNK

    def b_blocks(b):
        dense = jnp.where(same, jnp.tile(b, (S5_GROUPS, 1)), 0.0)
        return jnp.stack([dense[m * ch_b:(m + 1) * ch_b, m * st_b:(m + 1) * st_b] for m in range(S5_BLOCKS)])

    def c_blocks(c):
        c_sp = jnp.transpose(c, (0, 2, 1)).reshape(S5_FLAT, S5_GROUP_CH)
        dense = jnp.where(same.T, jnp.tile(c_sp, (1, S5_GROUPS)), 0.0)
        return jnp.stack([dense[m * st_b:(m + 1) * st_b, m * ch_b:(m + 1) * ch_b] for m in range(S5_BLOCKS)])

    b_blk = jnp.stack([b_blocks(bb[0]), b_blocks(bb[1])]).astype(BF16)
    c_blk = jnp.stack([c_blocks(c_re), c_blocks(c_im)]).astype(BF16)
    return b_blk, c_blk


def _inproj_kernel(x_ref, sc_ref, sh_ref, w_ref, *rest):
    pu_ref, k_ref, v_ref, bf_ref = rest[-4:]
    bb, tl, _ = x_ref.shape
    for bs, ts in _row_chunks(bb, tl):
        x = x_ref[bs, ts, :]
        cb, ct, _ = x.shape
        h = (_layer_norm(x) * (1.0 + sc_ref[bs]) + sh_ref[bs]).reshape(cb * ct, D_MODEL).astype(BF16)

        def cols(j, h=h, cb=cb, ct=ct):
            r = jnp.dot(h, w_ref[:, j * COL_TILE:(j + 1) * COL_TILE], preferred_element_type=F32)
            return r.reshape(cb, ct, COL_TILE)

        pu_ref[bs, ts, :] = cols(0)
        bf_ref[0, bs, ts, :] = cols(1).astype(BF16)
        k = cols(2)
        k_ref[0, bs, ts, :, :] = pltpu.einshape("bt(hd)->bthd", k, h=SB_HEADS)
        bf_ref[1, bs, ts, :] = k.astype(BF16)
        v = cols(3)
        v_ref[0, bs, ts, :, :] = pltpu.einshape("bt(hd)->bthd", v, h=SB_HEADS)
        bf_ref[2, bs, ts, :] = v.astype(BF16)


def _inproj(x, sc, sh, w_in, kv_prev, layer, bb, tl):
    B, L, _ = x.shape
    tok = lambda b, i: (b, i, 0)
    in_specs = [
        pl.BlockSpec((bb, tl, D_MODEL), tok),
        pl.BlockSpec((bb, 1, D_MODEL), lambda b, i: (b, 0, 0)),
        pl.BlockSpec((bb, 1, D_MODEL), lambda b, i: (b, 0, 0)),
        pl.BlockSpec((None, D_MODEL, IN_WIDTH), lambda b, i: (layer, 0, 0), pipeline_mode=pl.Buffered(1)),
    ]
    args = [x, sc, sh, w_in]
    aliases = {}
    if kv_prev is not None:
        in_specs += [pl.BlockSpec(memory_space=pl.ANY)] * 2
        args += list(kv_prev)
        aliases = {4: 1, 5: 2}
    kv_spec = pl.BlockSpec((1, bb, tl, SB_HEADS, SB_HEAD_DIM), lambda b, i: (layer, b, i, 0, 0))
    kv_shape = jax.ShapeDtypeStruct((DEPTH, B, L, SB_HEADS, SB_HEAD_DIM), F32)
    return pl.pallas_call(
        _inproj_kernel,
        grid=(B // bb, L // tl),
        in_specs=in_specs,
        out_specs=[
            pl.BlockSpec((bb, tl, COL_TILE), tok),
            kv_spec, kv_spec,
            pl.BlockSpec((3, bb, tl, COL_TILE), lambda b, i: (0, b, i, 0)),
        ],
        out_shape=[
            jax.ShapeDtypeStruct((B, L, COL_TILE), F32),
            kv_shape, kv_shape,
            jax.ShapeDtypeStruct((3, B, L, COL_TILE), BF16),
        ],
        input_output_aliases=aliases,
        compiler_params=_params("arbitrary", "arbitrary"),
        name="inproj",
    )(*args)


def _gelu_tanh(y):
    return 0.5 * y * (1.0 + jnp.tanh(math.sqrt(2.0 / math.pi) * (y + 0.044715 * (y * y * y))))


def _pools5_kernel(pos0, pu_ref, hist_ref, h0_ref, wpool_ref, pscale_ref, tab_ref, bblk_ref,
                   cblk_ref, dskip_ref, wglu_ref, mix_ref, nhist_ref, nstate_ref,
                   ext_ref, bu_ref, carry_ref):
    i = pl.program_id(1)
    T = pu_ref.shape[1]

    @pl.when(i == 0)
    def _():
        ext_ref[0:HIST_ROWS, :] = hist_ref[0]
        carry_ref[...] = h0_ref[0]

    p = pu_ref[0, :, 0:POOL_WIDTH]
    ext_ref[HIST_ROWS:HIST_ROWS + T, :] = p
    pos = pos0 + i * T + lax.broadcasted_iota(jnp.int32, (T, POOL_GROUP), 0)
    for g, w in enumerate(POOL_WINDOWS):
        lo, hi = g * POOL_GROUP, (g + 1) * POOL_GROUP
        win = ext_ref[HIST_ROWS:HIST_ROWS + T, lo:hi]
        for k in range(1, w):
            win = win + ext_ref[HIST_ROWS - k:HIST_ROWS - k + T, lo:hi]
        cnt = jnp.minimum(w, pos + 1).astype(F32)
        mixed = win / cnt - ext_ref[HIST_ROWS:HIST_ROWS + T, lo:hi]
        out = jnp.dot(mixed.astype(BF16), wpool_ref[g], preferred_element_type=F32)
        mix_ref[0, :, lo:hi] = (out * pscale_ref[:, lo:hi]).astype(BF16)
    last = ext_ref[T:T + HIST_ROWS, :]
    nhist_ref[0] = last
    ext_ref[0:HIST_ROWS, :] = last

    seg = T // SUBLANES
    shift = seg.bit_length() - 1
    n_idx = lax.broadcasted_iota(jnp.int32, (T, T), 0)
    t_idx = lax.broadcasted_iota(jnp.int32, (T, T), 1)
    perm = jnp.where(t_idx == (n_idx & (SUBLANES - 1)) * seg + (n_idx >> 3), 1.0, 0.0).astype(BF16)
    unperm = jnp.where(t_idx == (n_idx & (seg - 1)) * SUBLANES + (n_idx >> shift), 1.0, 0.0).astype(BF16)

    u = pu_ref[0, :, POOL_WIDTH:POOL_WIDTH + S5_WIDTH]
    u_hi = u.astype(BF16)
    u_lo = (u - u_hi.astype(F32)).astype(BF16)
    up_hi = jnp.dot(perm, u_hi, preferred_element_type=F32)
    u_perm = up_hi + jnp.dot(perm, u_lo, preferred_element_type=F32)
    ub = up_hi.astype(BF16)

    lc = S5_LANE_CHUNK
    row8 = lax.broadcasted_iota(jnp.int32, (SUBLANES, lc), 0)
    y_parts = []
    for m in range(S5_BLOCKS):
        re_l = slice(m * lc, (m + 1) * lc)
        im_l = slice(S5_FLAT + m * lc, S5_FLAT + (m + 1) * lc)
        ch_l = slice(m * S5_BLOCK_CH, (m + 1) * S5_BLOCK_CH)
        bu_ref[:, re_l] = jnp.dot(ub[:, ch_l], bblk_ref[0, m], preferred_element_type=F32)
        bu_ref[:, im_l] = jnp.dot(ub[:, ch_l], bblk_ref[1, m], preferred_element_type=F32)
        a_re = tab_ref[0, 0, :, re_l]
        a_im = tab_ref[0, 1, :, re_l]

        f_re = f_im = jnp.zeros((SUBLANES, lc), F32)
        for i in range(seg):
            rows = slice(i * SUBLANES, (i + 1) * SUBLANES)
            g_re, g_im = _cmul(a_re, a_im, f_re, f_im)
            f_re = g_re + bu_ref[rows, re_l]
            f_im = g_im + bu_ref[rows, im_l]
            bu_ref[rows, re_l] = f_re
            bu_ref[rows, im_l] = f_im

        for n, k in enumerate((1, 2, 4)):
            d_re, d_im = _cmul(tab_ref[0, 2 + 2 * n, :, re_l], tab_ref[0, 3 + 2 * n, :, re_l],
                               pltpu.roll(f_re, k, 0), pltpu.roll(f_im, k, 0))
            f_re, f_im = f_re + d_re, f_im + d_im
        c_re = carry_ref[:, re_l]
        c_im = carry_ref[:, im_l]
        e_re, e_im = _cmul(tab_ref[0, 8, :, re_l], tab_ref[0, 9, :, re_l], c_re, c_im)
        e_re = e_re + jnp.where(row8 >= 1, pltpu.roll(f_re, 1, 0), 0.0)
        e_im = e_im + jnp.where(row8 >= 1, pltpu.roll(f_im, 1, 0), 0.0)
        n_re, n_im = _cmul(tab_ref[0, 10, :, re_l], tab_ref[0, 11, :, re_l], c_re, c_im)
        carry_ref[:, re_l] = n_re + jnp.broadcast_to(f_re[SUBLANES - 1:SUBLANES, :], (SUBLANES, lc))
        carry_ref[:, im_l] = n_im + jnp.broadcast_to(f_im[SUBLANES - 1:SUBLANES, :], (SUBLANES, lc))

        w_re, w_im = _cmul(a_re, a_im, e_re, e_im)
        for i in range(seg):
            rows = slice(i * SUBLANES, (i + 1) * SUBLANES)
            bu_ref[rows, re_l] = bu_ref[rows, re_l] + w_re
            bu_ref[rows, im_l] = bu_ref[rows, im_l] + w_im
            if i + 1 < seg:
                w_re, w_im = _cmul(a_re, a_im, w_re, w_im)

        y_parts.append(
            jnp.dot(bu_ref[:, re_l].astype(BF16), cblk_ref[0, m], preferred_element_type=F32)
            - jnp.dot(bu_ref[:, im_l].astype(BF16), cblk_ref[1, m], preferred_element_type=F32))

    nstate_ref[0] = carry_ref[0:1, :]
    y = _gelu_tanh(jnp.concatenate(y_parts, axis=1) + dskip_ref[...] * u_perm)
    gate = jnp.dot(y.astype(BF16), wglu_ref[...], preferred_element_type=F32)
    s5_perm = (y * jax.nn.sigmoid(gate)).astype(BF16)
    mix_ref[0, :, POOL_WIDTH:POOL_WIDTH + S5_WIDTH] = jnp.dot(
        unperm, s5_perm, preferred_element_type=F32).astype(BF16)


def _pools5(pu, hist, h0, w_pool, pool_scale, tab, b_blk, c_blk, d_skip, w_glu, pos0, chunk):
    B, L, _ = pu.shape
    T = chunk
    const2 = lambda b, i: (0, 0)
    const4 = lambda b, i: (0, 0, 0, 0)
    return pl.pallas_call(
        functools.partial(_pools5_kernel, pos0),
        grid=(B, L // T),
        in_specs=[
            pl.BlockSpec((1, T, COL_TILE), lambda b, i: (b, i, 0)),
            pl.BlockSpec((1, HIST_ROWS, POOL_WIDTH), lambda b, i: (b, 0, 0)),
            pl.BlockSpec((1, SUBLANES, 2 * S5_FLAT), lambda b, i: (b, 0, 0)),
            pl.BlockSpec((len(POOL_WINDOWS), POOL_GROUP, POOL_GROUP), lambda b, i: (0, 0, 0)),
            pl.BlockSpec((1, POOL_WIDTH), const2),
            pl.BlockSpec((1, S5_TABLES, SUBLANES, S5_FLAT), const4),
            pl.BlockSpec((2, S5_BLOCKS, S5_BLOCK_CH, S5_LANE_CHUNK), const4),
            pl.BlockSpec((2, S5_BLOCKS, S5_LANE_CHUNK, S5_BLOCK_CH), const4),
            pl.BlockSpec((1, S5_WIDTH), const2),
            pl.BlockSpec((S5_WIDTH, S5_WIDTH), const2),
        ],
        out_specs=[
            pl.BlockSpec((1, T, COL_TILE), lambda b, i: (b, i, 0)),
            pl.BlockSpec((1, HIST_ROWS, POOL_WIDTH), lambda b, i: (b, 0, 0)),
            pl.BlockSpec((1, 1, 2 * S5_FLAT), lambda b, i: (b, 0, 0)),
        ],
        out_shape=[
            jax.ShapeDtypeStruct((B, L, COL_TILE), BF16),
            jax.ShapeDtypeStruct((B, HIST_ROWS, POOL_WIDTH), F32),
            jax.ShapeDtypeStruct((B, 1, 2 * S5_FLAT), F32),
        ],
        scratch_shapes=[
            pltpu.VMEM((HIST_ROWS + T, POOL_WIDTH), F32),
            pltpu.VMEM((T, 2 * S5_FLAT), F32),
            pltpu.VMEM((SUBLANES, 2 * S5_FLAT), F32),
        ],
        compiler_params=_params("arbitrary", "arbitrary"),
        name="pools5",
    )(pu, hist, h0, w_pool, pool_scale, tab, b_blk, c_blk, d_skip, w_glu)


def _tri_ones(n):
    j = lax.broadcasted_iota(jnp.int32, (n, 2 * n), 0)
    s = lax.broadcasted_iota(jnp.int32, (n, 2 * n), 1)
    return jnp.where((j >= s) | (s >= n), 1.0, 0.0).astype(BF16)


def _scores(q, k):
    return lax.dot_general(q, k, (((1,), (1,)), ((), ())),
                           preferred_element_type=F32) * (SB_HEAD_DIM ** -0.5 * LOG2_E)


def _stick_blocks(scores, vs, mask, laters, tri, stored_scores=False, after=None):
    n = vs[0].shape[0]
    n_pairs = len(vs)
    zs, sums, results = {}, {}, []
    for step in range(n_pairs + 2 * STAGE_SKEW):
        a, b, c = step, step - STAGE_SKEW, step - 2 * STAGE_SKEW
        if a < n_pairs and not stored_scores:
            zs[a] = scores[a]()
        if 0 <= b < n_pairs:
            z = scores[b]() if stored_scores else zs[b]
            sp = jnp.maximum(z, 0.0) + jnp.log2(1.0 + jnp.exp2(-jnp.abs(z)))
            if mask is not None:
                sp = jnp.where(mask, sp, 0.0)
            sums[b] = jnp.dot(sp.astype(BF16), tri, preferred_element_type=F32)
        if 0 <= c < n_pairs:
            s, later = sums.pop(c), laters[c]
            z = scores[c]() if stored_scores else zs.pop(c)
            w = jnp.exp2(z + (later - s[:, 0:n]))
            if mask is not None:
                w = jnp.where(mask, w, 0.0)
            results.append((jnp.dot(w.astype(BF16), vs[c], preferred_element_type=F32), later - s[:, n:2 * n]))
            if after is not None:
                after[c]()
    return results


def _attn_prompt_kernel(q_ref, k_ref, v_ref, o_ref, acc_ref, later_ref, z_ref):
    tile = pl.program_id(2)
    blk = ATT_BLOCK
    n_chain = q_ref.shape[2] // blk
    chains = range(n_chain)
    tri = _tri_ones(blk)
    row = lax.broadcasted_iota(jnp.int32, (blk, blk), 0)
    col = lax.broadcasted_iota(jnp.int32, (blk, blk), 1)
    causal = col < row
    zeros = jnp.zeros((blk, blk), F32)

    def key_rows(kb):
        return pl.ds(pl.multiple_of(jnp.maximum(kb, 0) * blk, blk), blk)

    def q_of(c):
        return q_ref[0, 0, c * blk:(c + 1) * blk, :]

    def score_ahead(c, kb):
        def run():
            z_ref[c] = _scores(q_of(c), k_ref[0, 0, key_rows(kb), :])
        return run

    def stored(c):
        return lambda: z_ref[c]

    diag = [tile * n_chain + c for c in chains]
    results = _stick_blocks([lambda c=c: _scores(q_of(c), k_ref[0, 0, key_rows(diag[c]), :]) for c in chains],
                            [v_ref[0, 0, key_rows(diag[c]), :] for c in chains], causal,
                            [zeros] * n_chain, tri, after=[score_ahead(c, diag[c] - 1) for c in chains])
    alive = None
    for c, (out, later) in enumerate(results):
        acc_ref[c] = out
        later_ref[c] = later
        alive = later if alive is None else jnp.maximum(alive, later)

    def cond(carry):
        t, go = carry
        return jnp.logical_and(t <= tile * n_chain + (n_chain - 1), go > 0)

    def body(carry):
        t, _ = carry
        kbs = [diag[c] - t for c in chains]
        laters = [jnp.where(kbs[c] < 0, STICK_PARKED, later_ref[c]) for c in chains]
        results = _stick_blocks([stored(c) for c in chains], [v_ref[0, 0, key_rows(kbs[c]), :] for c in chains],
                                None, laters, tri, stored_scores=True,
                                after=[score_ahead(c, kbs[c] - 1) for c in chains])
        alive = None
        for c, (out, later) in enumerate(results):
            acc_ref[c] += out
            later_ref[c] = later
            alive = later if alive is None else jnp.maximum(alive, later)
        return t + 1, (jnp.max(alive) >= STICK_LOG2_FLOOR).astype(jnp.int32)

    go = (jnp.max(alive) >= STICK_LOG2_FLOOR).astype(jnp.int32)
    lax.while_loop(cond, body, (jnp.int32(1), go))
    for c in chains:
        o_ref[0, c * blk:(c + 1) * blk, :] = acc_ref[c].astype(BF16)


def _attn_prompt(proj_bf):
    _, B, L, _ = proj_bf.shape
    tq = min(ATT_CHAINS * ATT_BLOCK, L)
    kv_spec = lambda part: pl.BlockSpec((1, 1, L, SB_HEAD_DIM), lambda b, h, i: (part, b, 0, h))
    return pl.pallas_call(
        _attn_prompt_kernel,
        grid=(B, SB_HEADS, L // tq),
        in_specs=[
            pl.BlockSpec((1, 1, tq, SB_HEAD_DIM), lambda b, h, i: (0, b, i, h)),
            kv_spec(1),
            kv_spec(2),
        ],
        out_specs=pl.BlockSpec((1, tq, SB_HEAD_DIM), lambda b, h, i: (b, i, h)),
        out_shape=jax.ShapeDtypeStruct((B, L, SB_WIDTH), BF16),
        scratch_shapes=[
            pltpu.VMEM((tq // ATT_BLOCK, ATT_BLOCK, ATT_BLOCK), F32),
            pltpu.VMEM((tq // ATT_BLOCK, ATT_BLOCK, ATT_BLOCK), F32),
            pltpu.VMEM((tq // ATT_BLOCK, ATT_BLOCK, ATT_BLOCK), F32),
        ],
        compiler_params=_params("arbitrary", "arbitrary", "arbitrary"),
        name="attn_prompt",
    )(proj_bf, proj_bf, proj_bf)


def _attn_sample_kernel(q_ref, kn_ref, vn_ref, kc_ref, vc_ref, o_ref):
    L = q_ref.shape[2]
    past = kc_ref.shape[2]
    cblk = min(ATT_BLOCK, past)
    tri_new = _tri_ones(L)
    tri_c = _tri_ones(cblk)
    row = lax.broadcasted_iota(jnp.int32, (L, L), 0)
    col = lax.broadcasted_iota(jnp.int32, (L, L), 1)
    causal = col < row
    lanes = [slice(h * SB_HEAD_DIM, (h + 1) * SB_HEAD_DIM) for h in range(SB_HEADS)]
    qs = [q_ref[0, 0, :, ln] for ln in lanes]
    results = _stick_blocks([lambda q=q, ln=ln: _scores(q, kn_ref[0, 0, :, ln]) for q, ln in zip(qs, lanes)],
                            [vn_ref[0, 0, :, ln] for ln in lanes], causal,
                            [jnp.zeros((L, L), F32)] * SB_HEADS, tri_new)
    accs = [out for out, _ in results]
    laters = [jnp.broadcast_to(later[:, 0:1], (L, cblk)) for _, later in results]
    for j in range(past // cblk - 1, -1, -1):
        rows = slice(j * cblk, (j + 1) * cblk)
        kc = pltpu.einshape("phd->p(hd)", kc_ref[0, 0, rows, :, :]).astype(BF16)
        vc = pltpu.einshape("phd->p(hd)", vc_ref[0, 0, rows, :, :]).astype(BF16)
        results = _stick_blocks([lambda q=q, ln=ln, kc=kc: _scores(q, kc[:, ln]) for q, ln in zip(qs, lanes)],
                                [vc[:, ln] for ln in lanes], None, laters, tri_c)
        accs = [acc + out for acc, (out, _) in zip(accs, results)]
        laters = [later for _, later in results]
    for ln, acc in zip(lanes, accs):
        o_ref[0, :, ln] = acc.astype(BF16)


def _attn_sample(proj_bf, cache_k, cache_v, layer):
    _, B, L, _ = proj_bf.shape
    past = cache_k.shape[2]
    new_spec = lambda part: pl.BlockSpec((1, 1, L, SB_WIDTH), lambda b: (part, b, 0, 0))
    cache_spec = pl.BlockSpec((1, 1, past, SB_HEADS, SB_HEAD_DIM), lambda b: (layer, b, 0, 0, 0))
    return pl.pallas_call(
        _attn_sample_kernel,
        grid=(B,),
        in_specs=[new_spec(0), new_spec(1), new_spec(2), cache_spec, cache_spec],
        out_specs=pl.BlockSpec((1, L, SB_WIDTH), lambda b: (b, 0, 0)),
        out_shape=jax.ShapeDtypeStruct((B, L, SB_WIDTH), BF16),
        compiler_params=_params("arbitrary"),
        name="attn_sample",
    )(proj_bf, proj_bf, proj_bf, cache_k, cache_v)


def _outproj_kernel(mix_ref, att_ref, x_ref, g_ref, sc_ref, sh_ref, w_ref, lng_ref, lnb_ref, o_ref, h_ref):
    bb, tl, _ = x_ref.shape
    half = mix_ref.shape[-1]
    for bs, ts in _row_chunks(bb, tl):
        m = mix_ref[bs, ts, :]
        cb, ct, _ = m.shape
        mix = (jnp.dot(m.reshape(cb * ct, half), w_ref[0:half, :], preferred_element_type=F32)
               + jnp.dot(att_ref[bs, ts, :].reshape(cb * ct, half), w_ref[half:2 * half, :],
                         preferred_element_type=F32))
        t = DEEPNORM_ALPHA * x_ref[bs, ts, :] + g_ref[bs] * mix.reshape(cb, ct, D_MODEL)
        x1 = _layer_norm(t) * lng_ref[...] + lnb_ref[...]
        o_ref[bs, ts, :] = x1
        h_ref[bs, ts, :] = (_layer_norm(x1) * (1.0 + sc_ref[bs]) + sh_ref[bs]).astype(BF16)


def _outproj(mix, att, x, gate, sc2, sh2, w_out, layer, ln_g, ln_b, bb, tl):
    B, L, _ = x.shape
    tok = lambda width: pl.BlockSpec((bb, tl, width), lambda b, i: (b, i, 0))
    mod = pl.BlockSpec((bb, 1, D_MODEL), lambda b, i: (b, 0, 0))
    vec = pl.BlockSpec((1, 1, D_MODEL), lambda b, i: (0, 0, 0))
    return pl.pallas_call(
        _outproj_kernel,
        grid=(B // bb, L // tl),
        in_specs=[
            tok(COL_TILE), tok(SB_WIDTH), tok(D_MODEL), mod, mod, mod,
            pl.BlockSpec((None, 2 * COL_TILE, D_MODEL), lambda b, i: (layer, 0, 0),
                         pipeline_mode=pl.Buffered(1)),
            vec, vec,
        ],
        out_specs=[tok(D_MODEL), tok(D_MODEL)],
        out_shape=[jax.ShapeDtypeStruct((B, L, D_MODEL), F32), jax.ShapeDtypeStruct((B, L, D_MODEL), BF16)],
        compiler_params=_params("arbitrary", "arbitrary"),
        name="outproj",
    )(mix, att, x, gate, sc2, sh2, w_out, ln_g.reshape(1, 1, D_MODEL), ln_b.reshape(1, 1, D_MODEL))


def _ffn_kernel(h_ref, x_ref, g_ref, wup_ref, wdn_ref, lng_ref, lnb_ref, o_ref, acc_ref):
    f = pl.program_id(2)
    bb, tl, _ = x_ref.shape
    rows = bb * tl

    def partial_sum():
        hid = jnp.dot(h_ref[...].reshape(rows, D_MODEL), wup_ref[...], preferred_element_type=F32)
        hid = jnp.square(jnp.maximum(hid, 0.0))
        return jnp.dot(hid.astype(BF16), wdn_ref[...], preferred_element_type=F32)

    @pl.when(f == 0)
    def _():
        acc_ref[...] = partial_sum()

    @pl.when(f > 0)
    def _():
        acc_ref[...] += partial_sum()

    @pl.when(f == pl.num_programs(2) - 1)
    def _():
        t = DEEPNORM_ALPHA * x_ref[...] + g_ref[...] * acc_ref[...].reshape(bb, tl, D_MODEL)
        o_ref[...] = _layer_norm(t) * lng_ref[...] + lnb_ref[...]


def _ffn(h2, x, gate, w_up, w_down, layer, ln_g, ln_b, bb, tl):
    B, L, _ = x.shape
    tf = FF_TILE
    tok = pl.BlockSpec((bb, tl, D_MODEL), lambda b, i, f: (b, i, 0))
    mod = pl.BlockSpec((bb, 1, D_MODEL), lambda b, i, f: (b, 0, 0))
    vec = pl.BlockSpec((1, 1, D_MODEL), lambda b, i, f: (0, 0, 0))
    return pl.pallas_call(
        _ffn_kernel,
        grid=(B // bb, L // tl, D_FF // tf),
        in_specs=[
            tok, tok, mod,
            pl.BlockSpec((None, D_MODEL, tf), lambda b, i, f: (layer, 0, f)),
            pl.BlockSpec((None, tf, D_MODEL), lambda b, i, f: (layer, f, 0)),
            vec, vec,
        ],
        out_specs=tok,
        out_shape=jax.ShapeDtypeStruct((B, L, D_MODEL), F32),
        scratch_shapes=[pltpu.VMEM((bb * tl, D_MODEL), F32)],
        compiler_params=_params("arbitrary", "arbitrary", "arbitrary"),
        name="ffn",
    )(h2, x, gate, w_up, w_down, ln_g.reshape(1, 1, D_MODEL), ln_b.reshape(1, 1, D_MODEL))


def _row_blocking(B, L):
    tl = min(L, ROW_TILE)
    bb = max(1, min(B, ROW_TILE // tl))
    return bb, tl


def _run_group(x, ada, hist, s5_re, s5_im, cache_k, cache_v, pos0, lw, s5_raw):
    B, L, _ = x.shape
    bb, tl = _row_blocking(B, L)
    chunk = min(L, S5_CHUNK)
    tab, _ = _s5prep(*s5_raw, chunk // SUBLANES)
    new_hist, new_re, new_im = [], [], []
    kv = None
    for l in range(DEPTH):
        w = lw[l]
        sh1, sc1, g1, sh2, sc2, g2 = [ada[l][:, None, n * D_MODEL:(n + 1) * D_MODEL] for n in range(6)]
        pu, k_all, v_all, proj_bf = _inproj(x, sc1, sh1, w["w_in"], kv, l, bb, tl)
        kv = (k_all, v_all)
        hist16 = jnp.pad(hist[l], ((0, 0), (HIST_ROWS - POOL_HIST, 0), (0, 0)))
        h0 = jnp.concatenate([s5_re[l].reshape(B, 1, S5_FLAT), s5_im[l].reshape(B, 1, S5_FLAT)], axis=-1)
        h0 = jnp.broadcast_to(h0, (B, SUBLANES, 2 * S5_FLAT))
        mix, nh, ns = _pools5(pu, hist16, h0, w["w_pool"], w["pool_scale"], tab[l:l + 1], w["b_blk"],
                              w["c_blk"], w["d_skip"], w["w_glu"], pos0, chunk)
        if cache_k is None:
            att = _attn_prompt(proj_bf)
        else:
            att = _attn_sample(proj_bf, cache_k, cache_v, l)
        x, h2 = _outproj(mix, att, x, g1, sc2, sh2, w["w_out"], l, w["ln1_g"], w["ln1_b"], bb, tl)
        x = _ffn(h2, x, g2, w["w_up"], w["w_down"], l, w["ln2_g"], w["ln2_b"], bb, tl)
        new_hist.append(nh[:, HIST_ROWS - POOL_HIST:, :])
        new_re.append(ns[:, 0, 0:S5_FLAT].reshape(B, S5_GROUPS, S5_STATE))
        new_im.append(ns[:, 0, S5_FLAT:].reshape(B, S5_GROUPS, S5_STATE))
    stack = lambda xs: jnp.stack(xs)
    return x, (stack(new_hist), stack(new_re), stack(new_im), kv[0], kv[1])


def kernel(x_prompt, x_sample, state_pool, state_s5_re, state_s5_im, cache_k, cache_v, c_prompt, c_sample,
           w_ada, b_ada, w_in, w_pool, pool_scale, s5_a_re, s5_a_im, s5_log_dt, s5_b_re, s5_b_im,
           s5_c_re, s5_c_im, s5_d, w_glu, w_out, ln1_g, ln1_b, w_up, w_down, ln2_g, ln2_b):
    B = x_prompt.shape[0]
    Bs = x_sample.shape[0]
    past = cache_k.shape[2]

    n_c = B + Bs
    pad = (-n_c) % 16
    c_all = jnp.concatenate([c_prompt, c_sample, jnp.zeros((pad, D_MODEL), F32)], axis=0)
    ada = _ada(c_all, w_ada, b_ada)
    ada_p, ada_s = ada[:, 0:B], ada[:, B:n_c]

    s5_raw = (s5_a_re, s5_a_im, s5_log_dt, s5_b_re, s5_b_im)
    _, bb_disc = _s5prep(*s5_raw, SUBLANES)
    w_in_bf, w_out_bf, w_up_bf, w_down_bf = [_to_bf16(w) for w in (w_in, w_out, w_up, w_down)]
    lw = []
    for l in range(DEPTH):
        b_blk, c_blk = _s5_block_weights(bb_disc[l], s5_c_re[l], s5_c_im[l])
        lw.append(dict(
            w_in=w_in_bf, w_out=w_out_bf, w_up=w_up_bf, w_down=w_down_bf,
            w_glu=w_glu[l].astype(BF16), w_pool=w_pool[l].astype(BF16),
            pool_scale=pool_scale[l].reshape(1, POOL_WIDTH),
            d_skip=s5_d[l].reshape(1, S5_WIDTH),
            b_blk=b_blk, c_blk=c_blk,
            ln1_g=ln1_g[l], ln1_b=ln1_b[l], ln2_g=ln2_g[l], ln2_b=ln2_b[l],
        ))

    zero_hist = jnp.zeros((DEPTH, B, POOL_HIST, POOL_WIDTH), F32)
    zero_s5 = jnp.zeros((DEPTH, B, S5_GROUPS, S5_STATE), F32)
    y_p, (pool_p, re_p, im_p, k_p, v_p) = _run_group(
        x_prompt, ada_p, zero_hist, zero_s5, zero_s5, None, None, 0, lw, s5_raw)
    y_s, (pool_s, re_s, im_s, k_s, v_s) = _run_group(
        x_sample, ada_s, state_pool, state_s5_re, state_s5_im, cache_k, cache_v, past, lw, s5_raw)
    return (y_p, y_s, pool_p, re_p, im_p, k_p, v_p, pool_s, re_s, im_s, k_s, v_s)
```

```python
import functools
import math

import jax
import jax.numpy as jnp
from jax import lax
from jax.experimental import pallas as pl
from jax.experimental.pallas import tpu as pltpu

F32 = jnp.float32
BF16 = jnp.bfloat16

D_MODEL = 2048
DEPTH = 2
POOL_WIDTH = 512
POOL_WINDOWS = (2, 4, 8, 16)
POOL_GROUP = 128
POOL_HIST = 15
HIST_ROWS = 16
S5_WIDTH = 512
S5_GROUP_CH = 16
S5_GROUPS = 32
S5_STATE = 64
S5_FLAT = S5_GROUPS * S5_STATE
SB_WIDTH = 1024
SB_HEAD_DIM = 128
SB_HEADS = 8
D_FF = 4 * D_MODEL
IN_WIDTH = 4096
COL_TILE = 1024
DEEPNORM_ALPHA = (2 * DEPTH) ** 0.25
LN_EPS = 1e-5

V7X_VMEM_LIMIT = 56 * 1024 * 1024
SUBLANES = 8
ROW_TILE = 512
ROW_CHUNKS = 4
FF_TILE = 1024
CAST_TILE_BYTES = 4 * 1024 * 1024
S5_CHUNK = 512
S5_LANE_CHUNK = 512
S5_BLOCKS = S5_FLAT // S5_LANE_CHUNK
S5_BLOCK_CH = S5_WIDTH // S5_BLOCKS
S5_TABLES = 12
ATT_BLOCK = 128
STAGE_SKEW = 4
ATT_CHAINS = 32
LOG2_E = 1.4426950408889634
STICK_LOG2_FLOOR = -150.0
STICK_PARKED = -1e30


def _params(*sem):
    return pltpu.CompilerParams(dimension_semantics=sem, vmem_limit_bytes=V7X_VMEM_LIMIT)


def _layer_norm(x):
    mu = jnp.mean(x, axis=-1, keepdims=True)
    xc = x - mu
    var = jnp.mean(xc * xc, axis=-1, keepdims=True)
    return xc * lax.rsqrt(var + LN_EPS)


def _row_chunks(bb, tl):
    if bb >= ROW_CHUNKS:
        step = bb // ROW_CHUNKS
        return [(slice(c * step, (c + 1) * step), slice(0, tl)) for c in range(ROW_CHUNKS)]
    step = tl // ROW_CHUNKS
    return [(slice(0, bb), slice(c * step, (c + 1) * step)) for c in range(ROW_CHUNKS)]


def _cmul(a_re, a_im, b_re, b_im):
    return a_re * b_re - a_im * b_im, a_re * b_im + a_im * b_re


def _cast_kernel(w_ref, o_ref):
    o_ref[...] = w_ref[...].astype(BF16)


def _to_bf16(w):
    depth, k, n = w.shape
    rows = max(16, min(k, CAST_TILE_BYTES // (4 * n)))
    assert k % rows == 0
    spec = pl.BlockSpec((1, rows, n), lambda l, i: (l, i, 0))
    return pl.pallas_call(
        _cast_kernel,
        grid=(depth, k // rows),
        in_specs=[spec],
        out_specs=spec,
        out_shape=jax.ShapeDtypeStruct(w.shape, BF16),
        compiler_params=_params("arbitrary", "arbitrary"),
        name="to_bf16",
    )(w)


def _ada_kernel(c_ref, w_ref, b_ref, o_ref):
    c = c_ref[...]
    s = (c * jax.nn.sigmoid(c)).astype(BF16)
    w = w_ref[0].astype(BF16)
    o_ref[0] = jnp.dot(s, w, preferred_element_type=F32) + b_ref[0]


def _ada(c_all, w_ada, b_ada):
    rows = c_all.shape[0]
    n_out = w_ada.shape[-1]
    tn = 1024
    return pl.pallas_call(
        _ada_kernel,
        grid=(DEPTH, n_out // tn),
        in_specs=[
            pl.BlockSpec((rows, D_MODEL), lambda l, j: (0, 0)),
            pl.BlockSpec((1, D_MODEL, tn), lambda l, j: (l, 0, j)),
            pl.BlockSpec((1, 1, tn), lambda l, j: (l, 0, j)),
        ],
        out_specs=pl.BlockSpec((1, rows, tn), lambda l, j: (l, 0, j)),
        out_shape=jax.ShapeDtypeStruct((DEPTH, rows, n_out), F32),
        compiler_params=_params("arbitrary", "arbitrary"),
        name="ada",
    )(c_all, w_ada, b_ada.reshape(DEPTH, 1, n_out))


def _s5prep_kernel(seg, are_ref, aim_ref, ldt_ref, bre_ref, bim_ref, tab_ref, bb_ref):
    a_re = are_ref[0]
    a_im = aim_ref[0]
    dt = jnp.exp(ldt_ref[0])
    shape = (SUBLANES, S5_FLAT)
    row = lax.broadcasted_iota(jnp.int32, shape, 0)
    mag = jnp.broadcast_to(jnp.exp(a_re * dt), shape)
    ang = jnp.broadcast_to(a_im * dt, shape)
    ab_re = mag * jnp.cos(ang)
    ab_im = mag * jnp.sin(ang)
    tab_ref[0, 0] = ab_re
    tab_ref[0, 1] = ab_im
    p_re, p_im = ab_re, ab_im
    for _ in range(seg.bit_length() - 1):
        p_re, p_im = _cmul(p_re, p_im, p_re, p_im)
    ps_re = jnp.ones(shape, F32)
    ps_im = jnp.zeros(shape, F32)
    for n, k in enumerate((1, 2, 4)):
        tab_ref[0, 2 + 2 * n] = jnp.where(row >= k, p_re, 0.0)
        tab_ref[0, 3 + 2 * n] = jnp.where(row >= k, p_im, 0.0)
        q_re, q_im = _cmul(ps_re, ps_im, p_re, p_im)
        has_bit = (row & k) != 0
        ps_re = jnp.where(has_bit, q_re, ps_re)
        ps_im = jnp.where(has_bit, q_im, ps_im)
        p_re, p_im = _cmul(p_re, p_im, p_re, p_im)
    tab_ref[0, 8] = ps_re
    tab_ref[0, 9] = ps_im
    tab_ref[0, 10] = p_re
    tab_ref[0, 11] = p_im
    a1_re = ab_re[0:1, :]
    a1_im = ab_im[0:1, :]
    den = a_re * a_re + a_im * a_im
    f_re = ((a1_re - 1.0) * a_re + a1_im * a_im) / den
    f_im = (a1_im * a_re - (a1_re - 1.0) * a_im) / den
    b_re = bre_ref[0]
    b_im = bim_ref[0]
    bb_ref[0, 0] = f_re * b_re - f_im * b_im
    bb_ref[0, 1] = f_re * b_im + f_im * b_re


def _s5prep(a_re, a_im, log_dt, b_re, b_im, seg):
    assert seg & (seg - 1) == 0
    flat = lambda a: a.reshape(DEPTH, 1, S5_FLAT)
    ldt = jnp.broadcast_to(log_dt[:, :, None], (DEPTH, S5_GROUPS, S5_STATE)).reshape(DEPTH, 1, S5_FLAT)
    to_ch_major = lambda b: jnp.transpose(b, (0, 3, 1, 2)).reshape(DEPTH, S5_GROUP_CH, S5_FLAT)
    row_spec = pl.BlockSpec((1, 1, S5_FLAT), lambda l: (l, 0, 0))
    b_spec = pl.BlockSpec((1, S5_GROUP_CH, S5_FLAT), lambda l: (l, 0, 0))
    return pl.pallas_call(
        functools.partial(_s5prep_kernel, seg),
        grid=(DEPTH,),
        in_specs=[row_spec, row_spec, row_spec, b_spec, b_spec],
        out_specs=[
            pl.BlockSpec((1, S5_TABLES, SUBLANES, S5_FLAT), lambda l: (l, 0, 0, 0)),
            pl.BlockSpec((1, 2, S5_GROUP_CH, S5_FLAT), lambda l: (l, 0, 0, 0)),
        ],
        out_shape=[
            jax.ShapeDtypeStruct((DEPTH, S5_TABLES, SUBLANES, S5_FLAT), F32),
            jax.ShapeDtypeStruct((DEPTH, 2, S5_GROUP_CH, S5_FLAT), F32),
        ],
        compiler_params=_params("arbitrary"),
        name="s5prep",
    )(flat(a_re), flat(a_im), ldt, to_ch_major(b_re), to_ch_major(b_im))


def _s5_block_weights(bb, c_re, c_im):
    ch_group = jnp.arange(S5_WIDTH)[:, None] // S5_GROUP_CH
    st_group = jnp.arange(S5_FLAT)[None, :] // S5_STATE
    same = ch_group == st_group
    ch_b, st_b = S5_BLOCK_CH, S5_LANE_CHUNK

    def b_blocks(b):
        dense = jnp.where(same, jnp.tile(b, (S5_GROUPS, 1)), 0.0)
        return jnp.stack([dense[m * ch_b:(m + 1) * ch_b, m * st_b:(m + 1) * st_b] for m in range(S5_BLOCKS)])

    def c_blocks(c):
        c_sp = jnp.transpose(c, (0, 2, 1)).reshape(S5_FLAT, S5_GROUP_CH)
        dense = jnp.where(same.T, jnp.tile(c_sp, (1, S5_GROUPS)), 0.0)
        return jnp.stack([dense[m * st_b:(m + 1) * st_b, m * ch_b:(m + 1) * ch_b] for m in range(S5_BLOCKS)])

    b_blk = jnp.stack([b_blocks(bb[0]), b_blocks(bb[1])]).astype(BF16)
    c_blk = jnp.stack([c_blocks(c_re), c_blocks(c_im)]).astype(BF16)
    return b_blk, c_blk


def _inproj_kernel(slot, x_ref, sc_ref, sh_ref, w_ref, *rest):
    pu_ref, k_ref, v_ref, bf_ref = rest[-4:]
    bb, tl, _ = x_ref.shape
    for other in range(k_ref.shape[0]):
        if other != slot:
            k_ref[other] = jnp.zeros(k_ref.shape[1:], F32)
            v_ref[other] = jnp.zeros(v_ref.shape[1:], F32)
    for bs, ts in _row_chunks(bb, tl):
        x = x_ref[bs, ts, :]
        cb, ct, _ = x.shape
        h = (_layer_norm(x) * (1.0 + sc_ref[bs]) + sh_ref[bs]).reshape(cb * ct, D_MODEL).astype(BF16)

        def cols(j, h=h, cb=cb, ct=ct):
            r = jnp.dot(h, w_ref[:, j * COL_TILE:(j + 1) * COL_TILE], preferred_element_type=F32)
            return r.reshape(cb, ct, COL_TILE)

        pu_ref[bs, ts, :] = cols(0)
        bf_ref[0, bs, ts, :] = cols(1).astype(BF16)
        k = cols(2)
        k_ref[slot, bs, ts, :, :] = pltpu.einshape("bt(hd)->bthd", k, h=SB_HEADS)
        bf_ref[1, bs, ts, :] = k.astype(BF16)
        v = cols(3)
        v_ref[slot, bs, ts, :, :] = pltpu.einshape("bt(hd)->bthd", v, h=SB_HEADS)
        bf_ref[2, bs, ts, :] = v.astype(BF16)


def _inproj(x, sc, sh, w_in, kv_prev, layer, bb, tl):
    B, L, _ = x.shape
    tok = lambda b, i: (b, i, 0)
    in_specs = [
        pl.BlockSpec((bb, tl, D_MODEL), tok),
        pl.BlockSpec((bb, 1, D_MODEL), lambda b, i: (b, 0, 0)),
        pl.BlockSpec((bb, 1, D_MODEL), lambda b, i: (b, 0, 0)),
        pl.BlockSpec((None, D_MODEL, IN_WIDTH), lambda b, i: (layer, 0, 0), pipeline_mode=pl.Buffered(1)),
    ]
    args = [x, sc, sh, w_in]
    if kv_prev is None:
        aliases, slot = {}, layer
        kv_spec = pl.BlockSpec((DEPTH, bb, tl, SB_HEADS, SB_HEAD_DIM), lambda b, i: (0, b, i, 0, 0))
    else:
        in_specs += [pl.BlockSpec(memory_space=pl.ANY)] * 2
        args += list(kv_prev)
        aliases, slot = {4: 1, 5: 2}, 0
        kv_spec = pl.BlockSpec((1, bb, tl, SB_HEADS, SB_HEAD_DIM), lambda b, i: (layer, b, i, 0, 0))
    kv_shape = jax.ShapeDtypeStruct((DEPTH, B, L, SB_HEADS, SB_HEAD_DIM), F32)
    return pl.pallas_call(
        functools.partial(_inproj_kernel, slot),
        grid=(B // bb, L // tl),
        in_specs=in_specs,
        out_specs=[
            pl.BlockSpec((bb, tl, COL_TILE), tok),
            kv_spec, kv_spec,
            pl.BlockSpec((3, bb, tl, COL_TILE), lambda b, i: (0, b, i, 0)),
        ],
        out_shape=[
            jax.ShapeDtypeStruct((B, L, COL_TILE), F32),
            kv_shape, kv_shape,
            jax.ShapeDtypeStruct((3, B, L, COL_TILE), BF16),
        ],
        input_output_aliases=aliases,
        compiler_params=_params("arbitrary", "arbitrary"),
        name="inproj",
    )(*args)


def _gelu_tanh(y):
    return 0.5 * y * (1.0 + jnp.tanh(math.sqrt(2.0 / math.pi) * (y + 0.044715 * (y * y * y))))


def _pools5_kernel(pos0, pu_ref, hist_ref, h0_ref, wpool_ref, pscale_ref, tab_ref, bblk_ref,
                   cblk_ref, dskip_ref, wglu_ref, mix_ref, nhist_ref, nstate_ref,
                   ext_ref, bu_ref, carry_ref):
    i = pl.program_id(1)
    T = pu_ref.shape[1]

    @pl.when(i == 0)
    def _():
        ext_ref[0:HIST_ROWS, :] = hist_ref[0]
        carry_ref[...] = h0_ref[0]

    p = pu_ref[0, :, 0:POOL_WIDTH]
    ext_ref[HIST_ROWS:HIST_ROWS + T, :] = p
    pos = pos0 + i * T + lax.broadcasted_iota(jnp.int32, (T, POOL_GROUP), 0)
    for g, w in enumerate(POOL_WINDOWS):
        lo, hi = g * POOL_GROUP, (g + 1) * POOL_GROUP
        win = ext_ref[HIST_ROWS:HIST_ROWS + T, lo:hi]
        for k in range(1, w):
            win = win + ext_ref[HIST_ROWS - k:HIST_ROWS - k + T, lo:hi]
        cnt = jnp.minimum(w, pos + 1).astype(F32)
        mixed = win / cnt - ext_ref[HIST_ROWS:HIST_ROWS + T, lo:hi]
        out = jnp.dot(mixed.astype(BF16), wpool_ref[g], preferred_element_type=F32)
        mix_ref[0, :, lo:hi] = (out * pscale_ref[:, lo:hi]).astype(BF16)
    last = ext_ref[T:T + HIST_ROWS, :]
    nhist_ref[0] = last
    ext_ref[0:HIST_ROWS, :] = last

    seg = T // SUBLANES
    shift = seg.bit_length() - 1
    n_idx = lax.broadcasted_iota(jnp.int32, (T, T), 0)
    t_idx = lax.broadcasted_iota(jnp.int32, (T, T), 1)
    perm = jnp.where(t_idx == (n_idx & (SUBLANES - 1)) * seg + (n_idx >> 3), 1.0, 0.0).astype(BF16)
    unperm = jnp.where(t_idx == (n_idx & (seg - 1)) * SUBLANES + (n_idx >> shift), 1.0, 0.0).astype(BF16)

    u = pu_ref[0, :, POOL_WIDTH:POOL_WIDTH + S5_WIDTH]
    u_hi = u.astype(BF16)
    u_lo = (u - u_hi.astype(F32)).astype(BF16)
    up_hi = jnp.dot(perm, u_hi, preferred_element_type=F32)
    u_perm = up_hi + jnp.dot(perm, u_lo, preferred_element_type=F32)
    ub = up_hi.astype(BF16)

    lc = S5_LANE_CHUNK
    row8 = lax.broadcasted_iota(jnp.int32, (SUBLANES, lc), 0)
    y_parts = []
    for m in range(S5_BLOCKS):
        re_l = slice(m * lc, (m + 1) * lc)
        im_l = slice(S5_FLAT + m * lc, S5_FLAT + (m + 1) * lc)
        ch_l = slice(m * S5_BLOCK_CH, (m + 1) * S5_BLOCK_CH)
        bu_ref[:, re_l] = jnp.dot(ub[:, ch_l], bblk_ref[0, m], preferred_element_type=F32)
        bu_ref[:, im_l] = jnp.dot(ub[:, ch_l], bblk_ref[1, m], preferred_element_type=F32)
        a_re = tab_ref[0, 0, :, re_l]
        a_im = tab_ref[0, 1, :, re_l]

        f_re = f_im = jnp.zeros((SUBLANES, lc), F32)
        for i in range(seg):
            rows = slice(i * SUBLANES, (i + 1) * SUBLANES)
            g_re, g_im = _cmul(a_re, a_im, f_re, f_im)
            f_re = g_re + bu_ref[rows, re_l]
            f_im = g_im + bu_ref[rows, im_l]
            bu_ref[rows, re_l] = f_re
            bu_ref[rows, im_l] = f_im

        for n, k in enumerate((1, 2, 4)):
            d_re, d_im = _cmul(tab_ref[0, 2 + 2 * n, :, re_l], tab_ref[0, 3 + 2 * n, :, re_l],
                               pltpu.roll(f_re, k, 0), pltpu.roll(f_im, k, 0))
            f_re, f_im = f_re + d_re, f_im + d_im
        c_re = carry_ref[:, re_l]
        c_im = carry_ref[:, im_l]
        e_re, e_im = _cmul(tab_ref[0, 8, :, re_l], tab_ref[0, 9, :, re_l], c_re, c_im)
        e_re = e_re + jnp.where(row8 >= 1, pltpu.roll(f_re, 1, 0), 0.0)
        e_im = e_im + jnp.where(row8 >= 1, pltpu.roll(f_im, 1, 0), 0.0)
        n_re, n_im = _cmul(tab_ref[0, 10, :, re_l], tab_ref[0, 11, :, re_l], c_re, c_im)
        carry_ref[:, re_l] = n_re + jnp.broadcast_to(f_re[SUBLANES - 1:SUBLANES, :], (SUBLANES, lc))
        carry_ref[:, im_l] = n_im + jnp.broadcast_to(f_im[SUBLANES - 1:SUBLANES, :], (SUBLANES, lc))

        w_re, w_im = _cmul(a_re, a_im, e_re, e_im)
        for i in range(seg):
            rows = slice(i * SUBLANES, (i + 1) * SUBLANES)
            bu_ref[rows, re_l] = bu_ref[rows, re_l] + w_re
            bu_ref[rows, im_l] = bu_ref[rows, im_l] + w_im
            if i + 1 < seg:
                w_re, w_im = _cmul(a_re, a_im, w_re, w_im)

        y_parts.append(
            jnp.dot(bu_ref[:, re_l].astype(BF16), cblk_ref[0, m], preferred_element_type=F32)
            - jnp.dot(bu_ref[:, im_l].astype(BF16), cblk_ref[1, m], preferred_element_type=F32))

    nstate_ref[0] = carry_ref[0:1, :]
    y = _gelu_tanh(jnp.concatenate(y_parts, axis=1) + dskip_ref[...] * u_perm)
    gate = jnp.dot(y.astype(BF16), wglu_ref[...], preferred_element_type=F32)
    s5_perm = (y * jax.nn.sigmoid(gate)).astype(BF16)
    mix_ref[0, :, POOL_WIDTH:POOL_WIDTH + S5_WIDTH] = jnp.dot(
        unperm, s5_perm, preferred_element_type=F32).astype(BF16)


def _pools5(pu, hist, h0, w_pool, pool_scale, tab, b_blk, c_blk, d_skip, w_glu, pos0, chunk):
    B, L, _ = pu.shape
    T = chunk
    const2 = lambda b, i: (0, 0)
    const4 = lambda b, i: (0, 0, 0, 0)
    return pl.pallas_call(
        functools.partial(_pools5_kernel, pos0),
        grid=(B, L // T),
        in_specs=[
            pl.BlockSpec((1, T, COL_TILE), lambda b, i: (b, i, 0)),
            pl.BlockSpec((1, HIST_ROWS, POOL_WIDTH), lambda b, i: (b, 0, 0)),
            pl.BlockSpec((1, SUBLANES, 2 * S5_FLAT), lambda b, i: (b, 0, 0)),
            pl.BlockSpec((len(POOL_WINDOWS), POOL_GROUP, POOL_GROUP), lambda b, i: (0, 0, 0)),
            pl.BlockSpec((1, POOL_WIDTH), const2),
            pl.BlockSpec((1, S5_TABLES, SUBLANES, S5_FLAT), const4),
            pl.BlockSpec((2, S5_BLOCKS, S5_BLOCK_CH, S5_LANE_CHUNK), const4),
            pl.BlockSpec((2, S5_BLOCKS, S5_LANE_CHUNK, S5_BLOCK_CH), const4),
            pl.BlockSpec((1, S5_WIDTH), const2),
            pl.BlockSpec((S5_WIDTH, S5_WIDTH), const2),
        ],
        out_specs=[
            pl.BlockSpec((1, T, COL_TILE), lambda b, i: (b, i, 0)),
            pl.BlockSpec((1, HIST_ROWS, POOL_WIDTH), lambda b, i: (b, 0, 0)),
            pl.BlockSpec((1, 1, 2 * S5_FLAT), lambda b, i: (b, 0, 0)),
        ],
        out_shape=[
            jax.ShapeDtypeStruct((B, L, COL_TILE), BF16),
            jax.ShapeDtypeStruct((B, HIST_ROWS, POOL_WIDTH), F32),
            jax.ShapeDtypeStruct((B, 1, 2 * S5_FLAT), F32),
        ],
        scratch_shapes=[
            pltpu.VMEM((HIST_ROWS + T, POOL_WIDTH), F32),
            pltpu.VMEM((T, 2 * S5_FLAT), F32),
            pltpu.VMEM((SUBLANES, 2 * S5_FLAT), F32),
        ],
        compiler_params=_params("arbitrary", "arbitrary"),
        name="pools5",
    )(pu, hist, h0, w_pool, pool_scale, tab, b_blk, c_blk, d_skip, w_glu)


def _tri_ones(n):
    j = lax.broadcasted_iota(jnp.int32, (n, 2 * n), 0)
    s = lax.broadcasted_iota(jnp.int32, (n, 2 * n), 1)
    return jnp.where((j >= s) | (s >= n), 1.0, 0.0).astype(BF16)


def _scores(q, k):
    return lax.dot_general(q, k, (((1,), (1,)), ((), ())),
                           preferred_element_type=F32) * (SB_HEAD_DIM ** -0.5 * LOG2_E)


def _stick_blocks(scores, vs, mask, laters, tri, stored_scores=False, after=None):
    n = vs[0].shape[0]
    n_pairs = len(vs)
    zs, sums, results = {}, {}, []
    for step in range(n_pairs + 2 * STAGE_SKEW):
        a, b, c = step, step - STAGE_SKEW, step - 2 * STAGE_SKEW
        if a < n_pairs and not stored_scores:
            zs[a] = scores[a]()
        if 0 <= b < n_pairs:
            z = scores[b]() if stored_scores else zs[b]
            sp = jnp.maximum(z, 0.0) + jnp.log2(1.0 + jnp.exp2(-jnp.abs(z)))
            if mask is not None:
                sp = jnp.where(mask, sp, 0.0)
            sums[b] = jnp.dot(sp.astype(BF16), tri, preferred_element_type=F32)
        if 0 <= c < n_pairs:
            s, later = sums.pop(c), laters[c]
            z = scores[c]() if stored_scores else zs.pop(c)
            w = jnp.exp2(z + (later - s[:, 0:n]))
            if mask is not None:
                w = jnp.where(mask, w, 0.0)
            results.append((jnp.dot(w.astype(BF16), vs[c], preferred_element_type=F32), later - s[:, n:2 * n]))
            if after is not None:
                after[c]()
    return results


def _attn_prompt_kernel(q_ref, k_ref, v_ref, o_ref, acc_ref, later_ref, z_ref):
    tile = pl.program_id(2)
    blk = ATT_BLOCK
    n_chain = q_ref.shape[2] // blk
    chains = range(n_chain)
    tri = _tri_ones(blk)
    row = lax.broadcasted_iota(jnp.int32, (blk, blk), 0)
    col = lax.broadcasted_iota(jnp.int32, (blk, blk), 1)
    causal = col < row
    zeros = jnp.zeros((blk, blk), F32)

    def key_rows(kb):
        return pl.ds(pl.multiple_of(jnp.maximum(kb, 0) * blk, blk), blk)

    def q_of(c):
        return q_ref[0, 0, c * blk:(c + 1) * blk, :]

    def score_ahead(c, kb):
        def run():
            z_ref[c] = _scores(q_of(c), k_ref[0, 0, key_rows(kb), :])
        return run

    def stored(c):
        return lambda: z_ref[c]

    diag = [tile * n_chain + c for c in chains]
    results = _stick_blocks([lambda c=c: _scores(q_of(c), k_ref[0, 0, key_rows(diag[c]), :]) for c in chains],
                            [v_ref[0, 0, key_rows(diag[c]), :] for c in chains], causal,
                            [zeros] * n_chain, tri, after=[score_ahead(c, diag[c] - 1) for c in chains])
    alive = None
    for c, (out, later) in enumerate(results):
        acc_ref[c] = out
        later_ref[c] = later
        alive = later if alive is None else jnp.maximum(alive, later)

    def cond(carry):
        t, go = carry
        return jnp.logical_and(t <= tile * n_chain + (n_chain - 1), go > 0)

    def body(carry):
        t, _ = carry
        kbs = [diag[c] - t for c in chains]
        laters = [jnp.where(kbs[c] < 0, STICK_PARKED, later_ref[c]) for c in chains]
        results = _stick_blocks([stored(c) for c in chains], [v_ref[0, 0, key_rows(kbs[c]), :] for c in chains],
                                None, laters, tri, stored_scores=True,
                                after=[score_ahead(c, kbs[c] - 1) for c in chains])
        alive = None
        for c, (out, later) in enumerate(results):
            acc_ref[c] += out
            later_ref[c] = later
            alive = later if alive is None else jnp.maximum(alive, later)
        return t + 1, (jnp.max(alive) >= STICK_LOG2_FLOOR).astype(jnp.int32)

    go = (jnp.max(alive) >= STICK_LOG2_FLOOR).astype(jnp.int32)
    lax.while_loop(cond, body, (jnp.int32(1), go))
    for c in chains:
        o_ref[0, c * blk:(c + 1) * blk, :] = acc_ref[c].astype(BF16)


def _attn_prompt(proj_bf):
    _, B, L, _ = proj_bf.shape
    tq = min(ATT_CHAINS * ATT_BLOCK, L)
    kv_spec = lambda part: pl.BlockSpec((1, 1, L, SB_HEAD_DIM), lambda b, h, i: (part, b, 0, h))
    return pl.pallas_call(
        _attn_prompt_kernel,
        grid=(B, SB_HEADS, L // tq),
        in_specs=[
            pl.BlockSpec((1, 1, tq, SB_HEAD_DIM), lambda b, h, i: (0, b, i, h)),
            kv_spec(1),
            kv_spec(2),
        ],
        out_specs=pl.BlockSpec((1, tq, SB_HEAD_DIM), lambda b, h, i: (b, i, h)),
        out_shape=jax.ShapeDtypeStruct((B, L, SB_WIDTH), BF16),
        scratch_shapes=[
            pltpu.VMEM((tq // ATT_BLOCK, ATT_BLOCK, ATT_BLOCK), F32),
            pltpu.VMEM((tq // ATT_BLOCK, ATT_BLOCK, ATT_BLOCK), F32),
            pltpu.VMEM((tq // ATT_BLOCK, ATT_BLOCK, ATT_BLOCK), F32),
        ],
        compiler_params=_params("arbitrary", "arbitrary", "arbitrary"),
        name="attn_prompt",
    )(proj_bf, proj_bf, proj_bf)


def _attn_sample_kernel(q_ref, kn_ref, vn_ref, kc_ref, vc_ref, o_ref):
    L = q_ref.shape[2]
    past = kc_ref.shape[2]
    cblk = min(ATT_BLOCK, past)
    tri_new = _tri_ones(L)
    tri_c = _tri_ones(cblk)
    row = lax.broadcasted_iota(jnp.int32, (L, L), 0)
    col = lax.broadcasted_iota(jnp.int32, (L, L), 1)
    causal = col < row
    lanes = [slice(h * SB_HEAD_DIM, (h + 1) * SB_HEAD_DIM) for h in range(SB_HEADS)]
    qs = [q_ref[0, 0, :, ln] for ln in lanes]
    results = _stick_blocks([lambda q=q, ln=ln: _scores(q, kn_ref[0, 0, :, ln]) for q, ln in zip(qs, lanes)],
                            [vn_ref[0, 0, :, ln] for ln in lanes], causal,
                            [jnp.zeros((L, L), F32)] * SB_HEADS, tri_new)
    accs = [out for out, _ in results]
    laters = [jnp.broadcast_to(later[:, 0:1], (L, cblk)) for _, later in results]
    for j in range(past // cblk - 1, -1, -1):
        rows = slice(j * cblk, (j + 1) * cblk)
        kc = pltpu.einshape("phd->p(hd)", kc_ref[0, 0, rows, :, :]).astype(BF16)
        vc = pltpu.einshape("phd->p(hd)", vc_ref[0, 0, rows, :, :]).astype(BF16)
        results = _stick_blocks([lambda q=q, ln=ln, kc=kc: _scores(q, kc[:, ln]) for q, ln in zip(qs, lanes)],
                                [vc[:, ln] for ln in lanes], None, laters, tri_c)
        accs = [acc + out for acc, (out, _) in zip(accs, results)]
        laters = [later for _, later in results]
    for ln, acc in zip(lanes, accs):
        o_ref[0, :, ln] = acc.astype(BF16)


def _attn_sample(proj_bf, cache_k, cache_v, layer):
    _, B, L, _ = proj_bf.shape
    past = cache_k.shape[2]
    new_spec = lambda part: pl.BlockSpec((1, 1, L, SB_WIDTH), lambda b: (part, b, 0, 0))
    cache_spec = pl.BlockSpec((1, 1, past, SB_HEADS, SB_HEAD_DIM), lambda b: (layer, b, 0, 0, 0))
    return pl.pallas_call(
        _attn_sample_kernel,
        grid=(B,),
        in_specs=[new_spec(0), new_spec(1), new_spec(2), cache_spec, cache_spec],
        out_specs=pl.BlockSpec((1, L, SB_WIDTH), lambda b: (b, 0, 0)),
        out_shape=jax.ShapeDtypeStruct((B, L, SB_WIDTH), BF16),
        compiler_params=_params("arbitrary"),
        name="attn_sample",
    )(proj_bf, proj_bf, proj_bf, cache_k, cache_v)


def _outproj_kernel(mix_ref, att_ref, x_ref, g_ref, sc_ref, sh_ref, w_ref, lng_ref, lnb_ref, o_ref, h_ref):
    bb, tl, _ = x_ref.shape
    half = mix_ref.shape[-1]
    for bs, ts in _row_chunks(bb, tl):
        m = mix_ref[bs, ts, :]
        cb, ct, _ = m.shape
        mix = (jnp.dot(m.reshape(cb * ct, half), w_ref[0:half, :], preferred_element_type=F32)
               + jnp.dot(att_ref[bs, ts, :].reshape(cb * ct, half), w_ref[half:2 * half, :],
                         preferred_element_type=F32))
        t = DEEPNORM_ALPHA * x_ref[bs, ts, :] + g_ref[bs] * mix.reshape(cb, ct, D_MODEL)
        x1 = _layer_norm(t) * lng_ref[...] + lnb_ref[...]
        o_ref[bs, ts, :] = x1
        h_ref[bs, ts, :] = (_layer_norm(x1) * (1.0 + sc_ref[bs]) + sh_ref[bs]).astype(BF16)


def _outproj(mix, att, x, gate, sc2, sh2, w_out, layer, ln_g, ln_b, bb, tl):
    B, L, _ = x.shape
    tok = lambda width: pl.BlockSpec((bb, tl, width), lambda b, i: (b, i, 0))
    mod = pl.BlockSpec((bb, 1, D_MODEL), lambda b, i: (b, 0, 0))
    vec = pl.BlockSpec((1, 1, D_MODEL), lambda b, i: (0, 0, 0))
    return pl.pallas_call(
        _outproj_kernel,
        grid=(B // bb, L // tl),
        in_specs=[
            tok(COL_TILE), tok(SB_WIDTH), tok(D_MODEL), mod, mod, mod,
            pl.BlockSpec((None, 2 * COL_TILE, D_MODEL), lambda b, i: (layer, 0, 0),
                         pipeline_mode=pl.Buffered(1)),
            vec, vec,
        ],
        out_specs=[tok(D_MODEL), tok(D_MODEL)],
        out_shape=[jax.ShapeDtypeStruct((B, L, D_MODEL), F32), jax.ShapeDtypeStruct((B, L, D_MODEL), BF16)],
        compiler_params=_params("arbitrary", "arbitrary"),
        name="outproj",
    )(mix, att, x, gate, sc2, sh2, w_out, ln_g.reshape(1, 1, D_MODEL), ln_b.reshape(1, 1, D_MODEL))


def _ffn_kernel(h_ref, x_ref, g_ref, wup_ref, wdn_ref, lng_ref, lnb_ref, o_ref, acc_ref):
    f = pl.program_id(2)
    bb, tl, _ = x_ref.shape
    rows = bb * tl

    def partial_sum():
        hid = jnp.dot(h_ref[...].reshape(rows, D_MODEL), wup_ref[...], preferred_element_type=F32)
        hid = jnp.square(jnp.maximum(hid, 0.0))
        return jnp.dot(hid.astype(BF16), wdn_ref[...], preferred_element_type=F32)

    @pl.when(f == 0)
    def _():
        acc_ref[...] = partial_sum()

    @pl.when(f > 0)
    def _():
        acc_ref[...] += partial_sum()

    @pl.when(f == pl.num_programs(2) - 1)
    def _():
        t = DEEPNORM_ALPHA * x_ref[...] + g_ref[...] * acc_ref[...].reshape(bb, tl, D_MODEL)
        o_ref[...] = _layer_norm(t) * lng_ref[...] + lnb_ref[...]


def _ffn(h2, x, gate, w_up, w_down, layer, ln_g, ln_b, bb, tl):
    B, L, _ = x.shape
    tf = FF_TILE
    tok = pl.BlockSpec((bb, tl, D_MODEL), lambda b, i, f: (b, i, 0))
    mod = pl.BlockSpec((bb, 1, D_MODEL), lambda b, i, f: (b, 0, 0))
    vec = pl.BlockSpec((1, 1, D_MODEL), lambda b, i, f: (0, 0, 0))
    return pl.pallas_call(
        _ffn_kernel,
        grid=(B // bb, L // tl, D_FF // tf),
        in_specs=[
            tok, tok, mod,
            pl.BlockSpec((None, D_MODEL, tf), lambda b, i, f: (layer, 0, f)),
            pl.BlockSpec((None, tf, D_MODEL), lambda b, i, f: (layer, f, 0)),
            vec, vec,
        ],
        out_specs=tok,
        out_shape=jax.ShapeDtypeStruct((B, L, D_MODEL), F32),
        scratch_shapes=[pltpu.VMEM((bb * tl, D_MODEL), F32)],
        compiler_params=_params("arbitrary", "arbitrary", "arbitrary"),
        name="ffn",
    )(h2, x, gate, w_up, w_down, ln_g.reshape(1, 1, D_MODEL), ln_b.reshape(1, 1, D_MODEL))


def _row_blocking(B, L):
    tl = min(L, ROW_TILE)
    bb = max(1, min(B, ROW_TILE // tl))
    return bb, tl


def _run_group(x, ada, hist, s5_re, s5_im, cache_k, cache_v, pos0, lw, s5_raw):
    B, L, _ = x.shape
    bb, tl = _row_blocking(B, L)
    chunk = min(L, S5_CHUNK)
    tab, _ = _s5prep(*s5_raw, chunk // SUBLANES)
    new_hist, new_re, new_im = [], [], []
    kv = None
    for l in range(DEPTH):
        w = lw[l]
        sh1, sc1, g1, sh2, sc2, g2 = [ada[l][:, None, n * D_MODEL:(n + 1) * D_MODEL] for n in range(6)]
        pu, k_all, v_all, proj_bf = _inproj(x, sc1, sh1, w["w_in"], kv, l, bb, tl)
        kv = (k_all, v_all)
        hist16 = jnp.pad(hist[l], ((0, 0), (HIST_ROWS - POOL_HIST, 0), (0, 0)))
        h0 = jnp.concatenate([s5_re[l].reshape(B, 1, S5_FLAT), s5_im[l].reshape(B, 1, S5_FLAT)], axis=-1)
        h0 = jnp.broadcast_to(h0, (B, SUBLANES, 2 * S5_FLAT))
        mix, nh, ns = _pools5(pu, hist16, h0, w["w_pool"], w["pool_scale"], tab[l:l + 1], w["b_blk"],
                              w["c_blk"], w["d_skip"], w["w_glu"], pos0, chunk)
        if cache_k is None:
            att = _attn_prompt(proj_bf)
        else:
            att = _attn_sample(proj_bf, cache_k, cache_v, l)
        x, h2 = _outproj(mix, att, x, g1, sc2, sh2, w["w_out"], l, w["ln1_g"], w["ln1_b"], bb, tl)
        x = _ffn(h2, x, g2, w["w_up"], w["w_down"], l, w["ln2_g"], w["ln2_b"], bb, tl)
        new_hist.append(nh[:, HIST_ROWS - POOL_HIST:, :])
        new_re.append(ns[:, 0, 0:S5_FLAT].reshape(B, S5_GROUPS, S5_STATE))
        new_im.append(ns[:, 0, S5_FLAT:].reshape(B, S5_GROUPS, S5_STATE))
    stack = lambda xs: jnp.stack(xs)
    return x, (stack(new_hist), stack(new_re), stack(new_im), kv[0], kv[1])


def kernel(x_prompt, x_sample, state_pool, state_s5_re, state_s5_im, cache_k, cache_v, c_prompt, c_sample,
           w_ada, b_ada, w_in, w_pool, pool_scale, s5_a_re, s5_a_im, s5_log_dt, s5_b_re, s5_b_im,
           s5_c_re, s5_c_im, s5_d, w_glu, w_out, ln1_g, ln1_b, w_up, w_down, ln2_g, ln2_b):
    B = x_prompt.shape[0]
    Bs = x_sample.shape[0]
    past = cache_k.shape[2]

    n_c = B + Bs
    pad = (-n_c) % 16
    c_all = jnp.concatenate([c_prompt, c_sample, jnp.zeros((pad, D_MODEL), F32)], axis=0)
    ada = _ada(c_all, w_ada, b_ada)
    ada_p, ada_s = ada[:, 0:B], ada[:, B:n_c]

    s5_raw = (s5_a_re, s5_a_im, s5_log_dt, s5_b_re, s5_b_im)
    _, bb_disc = _s5prep(*s5_raw, SUBLANES)
    w_in_bf, w_out_bf, w_up_bf, w_down_bf = [_to_bf16(w) for w in (w_in, w_out, w_up, w_down)]
    lw = []
    for l in range(DEPTH):
        b_blk, c_blk = _s5_block_weights(bb_disc[l], s5_c_re[l], s5_c_im[l])
        lw.append(dict(
            w_in=w_in_bf, w_out=w_out_bf, w_up=w_up_bf, w_down=w_down_bf,
            w_glu=w_glu[l].astype(BF16), w_pool=w_pool[l].astype(BF16),
            pool_scale=pool_scale[l].reshape(1, POOL_WIDTH),
            d_skip=s5_d[l].reshape(1, S5_WIDTH),
            b_blk=b_blk, c_blk=c_blk,
            ln1_g=ln1_g[l], ln1_b=ln1_b[l], ln2_g=ln2_g[l], ln2_b=ln2_b[l],
        ))

    zero_hist = jnp.zeros((DEPTH, B, POOL_HIST, POOL_WIDTH), F32)
    zero_s5 = jnp.zeros((DEPTH, B, S5_GROUPS, S5_STATE), F32)
    y_p, (pool_p, re_p, im_p, k_p, v_p) = _run_group(
        x_prompt, ada_p, zero_hist, zero_s5, zero_s5, None, None, 0, lw, s5_raw)
    y_s, (pool_s, re_s, im_s, k_s, v_s) = _run_group(
        x_sample, ada_s, state_pool, state_s5_re, state_s5_im, cache_k, cache_v, past, lw, s5_raw)
    return (y_p, y_s, pool_p, re_p, im_p, k_p, v_p, pool_s, re_s, im_s, k_s, v_s)
```

```python
import functools
import math

import jax
import jax.numpy as jnp
from jax import lax
from jax.experimental import pallas as pl
from jax.experimental.pallas import tpu as pltpu

F32 = jnp.float32
BF16 = jnp.bfloat16

D_MODEL = 2048
DEPTH = 2
POOL_WIDTH = 512
POOL_WINDOWS = (2, 4, 8, 16)
POOL_GROUP = 128
POOL_HIST = 15
HIST_ROWS = 16
S5_WIDTH = 512
S5_GROUP_CH = 16
S5_GROUPS = 32
S5_STATE = 64
S5_FLAT = S5_GROUPS * S5_STATE
SB_WIDTH = 1024
SB_HEAD_DIM = 128
SB_HEADS = 8
D_FF = 4 * D_MODEL
IN_WIDTH = 4096
COL_TILE = 1024
DEEPNORM_ALPHA = (2 * DEPTH) ** 0.25
LN_EPS = 1e-5

V7X_VMEM_LIMIT = 56 * 1024 * 1024
SUBLANES = 8
ROW_TILE = 512
ROW_CHUNKS = 4
FF_TILE = 1024
CAST_TILE_BYTES = 4 * 1024 * 1024
S5_CHUNK = 512
S5_LANE_CHUNK = 512
S5_BLOCKS = S5_FLAT // S5_LANE_CHUNK
S5_BLOCK_CH = S5_WIDTH // S5_BLOCKS
S5_TABLES = 12
ATT_BLOCK = 128
STAGE_SKEW = 4
ATT_CHAINS = 32
LOG2_E = 1.4426950408889634
STICK_LOG2_FLOOR = -150.0
STICK_PARKED = -1e30


def _params(*sem):
    return pltpu.CompilerParams(dimension_semantics=sem, vmem_limit_bytes=V7X_VMEM_LIMIT)


def _layer_norm(x):
    mu = jnp.mean(x, axis=-1, keepdims=True)
    xc = x - mu
    var = jnp.mean(xc * xc, axis=-1, keepdims=True)
    return xc * lax.rsqrt(var + LN_EPS)


def _row_chunks(bb, tl):
    if bb >= ROW_CHUNKS:
        step = bb // ROW_CHUNKS
        return [(slice(c * step, (c + 1) * step), slice(0, tl)) for c in range(ROW_CHUNKS)]
    step = tl // ROW_CHUNKS
    return [(slice(0, bb), slice(c * step, (c + 1) * step)) for c in range(ROW_CHUNKS)]


def _cmul(a_re, a_im, b_re, b_im):
    return a_re * b_re - a_im * b_im, a_re * b_im + a_im * b_re


def _cast_kernel(w_ref, o_ref):
    o_ref[...] = w_ref[...].astype(BF16)


def _to_bf16(w):
    depth, k, n = w.shape
    rows = max(16, min(k, CAST_TILE_BYTES // (4 * n)))
    assert k % rows == 0
    spec = pl.BlockSpec((1, rows, n), lambda l, i: (l, i, 0))
    return pl.pallas_call(
        _cast_kernel,
        grid=(depth, k // rows),
        in_specs=[spec],
        out_specs=spec,
        out_shape=jax.ShapeDtypeStruct(w.shape, BF16),
        compiler_params=_params("arbitrary", "arbitrary"),
        name="to_bf16",
    )(w)


def _ada_kernel(c_ref, w_ref, b_ref, o_ref):
    c = c_ref[...]
    s = (c * jax.nn.sigmoid(c)).astype(BF16)
    w = w_ref[0].astype(BF16)
    o_ref[0] = jnp.dot(s, w, preferred_element_type=F32) + b_ref[0]


def _ada(c_all, w_ada, b_ada):
    rows = c_all.shape[0]
    n_out = w_ada.shape[-1]
    tn = 1024
    return pl.pallas_call(
        _ada_kernel,
        grid=(DEPTH, n_out // tn),
        in_specs=[
            pl.BlockSpec((rows, D_MODEL), lambda l, j: (0, 0)),
            pl.BlockSpec((1, D_MODEL, tn), lambda l, j: (l, 0, j)),
            pl.BlockSpec((1, 1, tn), lambda l, j: (l, 0, j)),
        ],
        out_specs=pl.BlockSpec((1, rows, tn), lambda l, j: (l, 0, j)),
        out_shape=jax.ShapeDtypeStruct((DEPTH, rows, n_out), F32),
        compiler_params=_params("arbitrary", "arbitrary"),
        name="ada",
    )(c_all, w_ada, b_ada.reshape(DEPTH, 1, n_out))


def _s5prep_kernel(seg, are_ref, aim_ref, ldt_ref, bre_ref, bim_ref, tab_ref, bb_ref):
    a_re = are_ref[0]
    a_im = aim_ref[0]
    dt = jnp.exp(ldt_ref[0])
    shape = (SUBLANES, S5_FLAT)
    row = lax.broadcasted_iota(jnp.int32, shape, 0)
    mag = jnp.broadcast_to(jnp.exp(a_re * dt), shape)
    ang = jnp.broadcast_to(a_im * dt, shape)
    ab_re = mag * jnp.cos(ang)
    ab_im = mag * jnp.sin(ang)
    tab_ref[0, 0] = ab_re
    tab_ref[0, 1] = ab_im
    p_re, p_im = ab_re, ab_im
    for _ in range(seg.bit_length() - 1):
        p_re, p_im = _cmul(p_re, p_im, p_re, p_im)
    ps_re = jnp.ones(shape, F32)
    ps_im = jnp.zeros(shape, F32)
    for n, k in enumerate((1, 2, 4)):
        tab_ref[0, 2 + 2 * n] = jnp.where(row >= k, p_re, 0.0)
        tab_ref[0, 3 + 2 * n] = jnp.where(row >= k, p_im, 0.0)
        q_re, q_im = _cmul(ps_re, ps_im, p_re, p_im)
        has_bit = (row & k) != 0
        ps_re = jnp.where(has_bit, q_re, ps_re)
        ps_im = jnp.where(has_bit, q_im, ps_im)
        p_re, p_im = _cmul(p_re, p_im, p_re, p_im)
    tab_ref[0, 8] = ps_re
    tab_ref[0, 9] = ps_im
    tab_ref[0, 10] = p_re
    tab_ref[0, 11] = p_im
    a1_re = ab_re[0:1, :]
    a1_im = ab_im[0:1, :]
    den = a_re * a_re + a_im * a_im
    f_re = ((a1_re - 1.0) * a_re + a1_im * a_im) / den
    f_im = (a1_im * a_re - (a1_re - 1.0) * a_im) / den
    b_re = bre_ref[0]
    b_im = bim_ref[0]
    bb_ref[0, 0] = f_re * b_re - f_im * b_im
    bb_ref[0, 1] = f_re * b_im + f_im * b_re


def _s5prep(a_re, a_im, log_dt, b_re, b_im, seg):
    assert seg & (seg - 1) == 0
    flat = lambda a: a.reshape(DEPTH, 1, S5_FLAT)
    ldt = jnp.broadcast_to(log_dt[:, :, None], (DEPTH, S5_GROUPS, S5_STATE)).reshape(DEPTH, 1, S5_FLAT)
    to_ch_major = lambda b: jnp.transpose(b, (0, 3, 1, 2)).reshape(DEPTH, S5_GROUP_CH, S5_FLAT)
    row_spec = pl.BlockSpec((1, 1, S5_FLAT), lambda l: (l, 0, 0))
    b_spec = pl.BlockSpec((1, S5_GROUP_CH, S5_FLAT), lambda l: (l, 0, 0))
    return pl.pallas_call(
        functools.partial(_s5prep_kernel, seg),
        grid=(DEPTH,),
        in_specs=[row_spec, row_spec, row_spec, b_spec, b_spec],
        out_specs=[
            pl.BlockSpec((1, S5_TABLES, SUBLANES, S5_FLAT), lambda l: (l, 0, 0, 0)),
            pl.BlockSpec((1, 2, S5_GROUP_CH, S5_FLAT), lambda l: (l, 0, 0, 0)),
        ],
        out_shape=[
            jax.ShapeDtypeStruct((DEPTH, S5_TABLES, SUBLANES, S5_FLAT), F32),
            jax.ShapeDtypeStruct((DEPTH, 2, S5_GROUP_CH, S5_FLAT), F32),
        ],
        compiler_params=_params("arbitrary"),
        name="s5prep",
    )(flat(a_re), flat(a_im), ldt, to_ch_major(b_re), to_ch_major(b_im))


def _s5_block_weights(bb, c_re, c_im):
    ch_group = jnp.arange(S5_WIDTH)[:, None] // S5_GROUP_CH
    st_group = jnp.arange(S5_FLAT)[None, :] // S5_STATE
    same = ch_group == st_group
    ch_b, st_b = S5_BLOCK_CH, S5_LANE_CHUNK

    def b_blocks(b):
        dense = jnp.where(same, jnp.tile(b, (S5_GROUPS, 1)), 0.0)
        return jnp.stack([dense[m * ch_b:(m + 1) * ch_b, m * st_b:(m + 1) * st_b] for m in range(S5_BLOCKS)])

    def c_blocks(c):
        c_sp = jnp.transpose(c, (0, 2, 1)).reshape(S5_FLAT, S5_GROUP_CH)
        dense = jnp.where(same.T, jnp.tile(c_sp, (1, S5_GROUPS)), 0.0)
        return jnp.stack([dense[m * st_b:(m + 1) * st_b, m * ch_b:(m + 1) * ch_b] for m in range(S5_BLOCKS)])

    b_blk = jnp.stack([b_blocks(bb[0]), b_blocks(bb[1])]).astype(BF16)
    c_blk = jnp.stack([c_blocks(c_re), c_blocks(c_im)]).astype(BF16)
    return b_blk, c_blk


def _inproj_kernel(slot, x_ref, sc_ref, sh_ref, w_ref, *rest):
    pu_ref, k_ref, v_ref, bf_ref = rest[-4:]
    bb, tl, _ = x_ref.shape
    for other in range(k_ref.shape[0]):
        if other != slot:
            k_ref[other] = jnp.zeros(k_ref.shape[1:], F32)
            v_ref[other] = jnp.zeros(v_ref.shape[1:], F32)
    for bs, ts in _row_chunks(bb, tl):
        x = x_ref[bs, ts, :]
        cb, ct, _ = x.shape
        h = (_layer_norm(x) * (1.0 + sc_ref[bs]) + sh_ref[bs]).reshape(cb * ct, D_MODEL).astype(BF16)

        def cols(j, h=h, cb=cb, ct=ct):
            r = jnp.dot(h, w_ref[:, j * COL_TILE:(j + 1) * COL_TILE], preferred_element_type=F32)
            return r.reshape(cb, ct, COL_TILE)

        pu_ref[bs, ts, :] = cols(0)
        bf_ref[0, bs, ts, :] = cols(1).astype(BF16)
        k = cols(2)
        k_ref[slot, bs, ts, :, :] = pltpu.einshape("bt(hd)->bthd", k, h=SB_HEADS)
        bf_ref[1, bs, ts, :] = k.astype(BF16)
        v = cols(3)
        v_ref[slot, bs, ts, :, :] = pltpu.einshape("bt(hd)->bthd", v, h=SB_HEADS)
        bf_ref[2, bs, ts, :] = v.astype(BF16)


def _inproj(x, sc, sh, w_in, kv_prev, layer, bb, tl):
    B, L, _ = x.shape
    tok = lambda b, i: (b, i, 0)
    in_specs = [
        pl.BlockSpec((bb, tl, D_MODEL), tok),
        pl.BlockSpec((bb, 1, D_MODEL), lambda b, i: (b, 0, 0)),
        pl.BlockSpec((bb, 1, D_MODEL), lambda b, i: (b, 0, 0)),
        pl.BlockSpec((None, D_MODEL, IN_WIDTH), lambda b, i: (layer, 0, 0), pipeline_mode=pl.Buffered(1)),
    ]
    args = [x, sc, sh, w_in]
    if kv_prev is None:
        aliases, slot = {}, layer
        kv_spec = pl.BlockSpec((DEPTH, bb, tl, SB_HEADS, SB_HEAD_DIM), lambda b, i: (0, b, i, 0, 0))
    else:
        in_specs += [pl.BlockSpec(memory_space=pl.ANY)] * 2
        args += list(kv_prev)
        aliases, slot = {4: 1, 5: 2}, 0
        kv_spec = pl.BlockSpec((1, bb, tl, SB_HEADS, SB_HEAD_DIM), lambda b, i: (layer, b, i, 0, 0))
    kv_shape = jax.ShapeDtypeStruct((DEPTH, B, L, SB_HEADS, SB_HEAD_DIM), F32)
    return pl.pallas_call(
        functools.partial(_inproj_kernel, slot),
        grid=(B // bb, L // tl),
        in_specs=in_specs,
        out_specs=[
            pl.BlockSpec((bb, tl, COL_TILE), tok),
            kv_spec, kv_spec,
            pl.BlockSpec((3, bb, tl, COL_TILE), lambda b, i: (0, b, i, 0)),
        ],
        out_shape=[
            jax.ShapeDtypeStruct((B, L, COL_TILE), F32),
            kv_shape, kv_shape,
            jax.ShapeDtypeStruct((3, B, L, COL_TILE), BF16),
        ],
        input_output_aliases=aliases,
        compiler_params=_params("arbitrary", "arbitrary"),
        name="inproj",
    )(*args)


def _gelu_tanh(y):
    return 0.5 * y * (1.0 + jnp.tanh(math.sqrt(2.0 / math.pi) * (y + 0.044715 * (y * y * y))))


def _pools5_kernel(pos0, pu_ref, hist_ref, h0_ref, wpool_ref, pscale_ref, tab_ref, bblk_ref,
                   cblk_ref, dskip_ref, wglu_ref, mix_ref, nhist_ref, nstate_ref,
                   ext_ref, bu_ref, carry_ref):
    i = pl.program_id(1)
    T = pu_ref.shape[1]

    @pl.when(i == 0)
    def _():
        ext_ref[0:HIST_ROWS, :] = hist_ref[0]
        carry_ref[...] = h0_ref[0]

    p = pu_ref[0, :, 0:POOL_WIDTH]
    ext_ref[HIST_ROWS:HIST_ROWS + T, :] = p
    pos = pos0 + i * T + lax.broadcasted_iota(jnp.int32, (T, POOL_GROUP), 0)
    for g, w in enumerate(POOL_WINDOWS):
        lo, hi = g * POOL_GROUP, (g + 1) * POOL_GROUP
        win = ext_ref[HIST_ROWS:HIST_ROWS + T, lo:hi]
        for k in range(1, w):
            win = win + ext_ref[HIST_ROWS - k:HIST_ROWS - k + T, lo:hi]
        cnt = jnp.minimum(w, pos + 1).astype(F32)
        mixed = win / cnt - ext_ref[HIST_ROWS:HIST_ROWS + T, lo:hi]
        out = jnp.dot(mixed.astype(BF16), wpool_ref[g], preferred_element_type=F32)
        mix_ref[0, :, lo:hi] = (out * pscale_ref[:, lo:hi]).astype(BF16)
    last = ext_ref[T:T + HIST_ROWS, :]
    nhist_ref[0] = last
    ext_ref[0:HIST_ROWS, :] = last

    seg = T // SUBLANES
    shift = seg.bit_length() - 1
    n_idx = lax.broadcasted_iota(jnp.int32, (T, T), 0)
    t_idx = lax.broadcasted_iota(jnp.int32, (T, T), 1)
    perm = jnp.where(t_idx == (n_idx & (SUBLANES - 1)) * seg + (n_idx >> 3), 1.0, 0.0).astype(BF16)
    unperm = jnp.where(t_idx == (n_idx & (seg - 1)) * SUBLANES + (n_idx >> shift), 1.0, 0.0).astype(BF16)

    u = pu_ref[0, :, POOL_WIDTH:POOL_WIDTH + S5_WIDTH]
    u_hi = u.astype(BF16)
    u_lo = (u - u_hi.astype(F32)).astype(BF16)
    up_hi = jnp.dot(perm, u_hi, preferred_element_type=F32)
    u_perm = up_hi + jnp.dot(perm, u_lo, preferred_element_type=F32)
    ub = up_hi.astype(BF16)

    lc = S5_LANE_CHUNK
    row8 = lax.broadcasted_iota(jnp.int32, (SUBLANES, lc), 0)
    y_parts = []
    for m in range(S5_BLOCKS):
        re_l = slice(m * lc, (m + 1) * lc)
        im_l = slice(S5_FLAT + m * lc, S5_FLAT + (m + 1) * lc)
        ch_l = slice(m * S5_BLOCK_CH, (m + 1) * S5_BLOCK_CH)
        bu_ref[:, re_l] = jnp.dot(ub[:, ch_l], bblk_ref[0, m], preferred_element_type=F32)
        bu_ref[:, im_l] = jnp.dot(ub[:, ch_l], bblk_ref[1, m], preferred_element_type=F32)
        a_re = tab_ref[0, 0, :, re_l]
        a_im = tab_ref[0, 1, :, re_l]

        f_re = f_im = jnp.zeros((SUBLANES, lc), F32)
        for i in range(seg):
            rows = slice(i * SUBLANES, (i + 1) * SUBLANES)
            g_re, g_im = _cmul(a_re, a_im, f_re, f_im)
            f_re = g_re + bu_ref[rows, re_l]
            f_im = g_im + bu_ref[rows, im_l]
            bu_ref[rows, re_l] = f_re
            bu_ref[rows, im_l] = f_im

        for n, k in enumerate((1, 2, 4)):
            d_re, d_im = _cmul(tab_ref[0, 2 + 2 * n, :, re_l], tab_ref[0, 3 + 2 * n, :, re_l],
                               pltpu.roll(f_re, k, 0), pltpu.roll(f_im, k, 0))
            f_re, f_im = f_re + d_re, f_im + d_im
        c_re = carry_ref[:, re_l]
        c_im = carry_ref[:, im_l]
        e_re, e_im = _cmul(tab_ref[0, 8, :, re_l], tab_ref[0, 9, :, re_l], c_re, c_im)
        e_re = e_re + jnp.where(row8 >= 1, pltpu.roll(f_re, 1, 0), 0.0)
        e_im = e_im + jnp.where(row8 >= 1, pltpu.roll(f_im, 1, 0), 0.0)
        n_re, n_im = _cmul(tab_ref[0, 10, :, re_l], tab_ref[0, 11, :, re_l], c_re, c_im)
        carry_ref[:, re_l] = n_re + jnp.broadcast_to(f_re[SUBLANES - 1:SUBLANES, :], (SUBLANES, lc))
        carry_ref[:, im_l] = n_im + jnp.broadcast_to(f_im[SUBLANES - 1:SUBLANES, :], (SUBLANES, lc))

        w_re, w_im = _cmul(a_re, a_im, e_re, e_im)
        for i in range(seg):
            rows = slice(i * SUBLANES, (i + 1) * SUBLANES)
            bu_ref[rows, re_l] = bu_ref[rows, re_l] + w_re
            bu_ref[rows, im_l] = bu_ref[rows, im_l] + w_im
            if i + 1 < seg:
                w_re, w_im = _cmul(a_re, a_im, w_re, w_im)

        y_parts.append(
            jnp.dot(bu_ref[:, re_l].astype(BF16), cblk_ref[0, m], preferred_element_type=F32)
            - jnp.dot(bu_ref[:, im_l].astype(BF16), cblk_ref[1, m], preferred_element_type=F32))

    nstate_ref[0] = carry_ref[0:1, :]
    y = _gelu_tanh(jnp.concatenate(y_parts, axis=1) + dskip_ref[...] * u_perm)
    gate = jnp.dot(y.astype(BF16), wglu_ref[...], preferred_element_type=F32)
    s5_perm = (y * jax.nn.sigmoid(gate)).astype(BF16)
    mix_ref[0, :, POOL_WIDTH:POOL_WIDTH + S5_WIDTH] = jnp.dot(
        unperm, s5_perm, preferred_element_type=F32).astype(BF16)


def _pools5(pu, hist, h0, w_pool, pool_scale, tab, b_blk, c_blk, d_skip, w_glu, pos0, chunk):
    B, L, _ = pu.shape
    T = chunk
    const2 = lambda b, i: (0, 0)
    const4 = lambda b, i: (0, 0, 0, 0)
    return pl.pallas_call(
        functools.partial(_pools5_kernel, pos0),
        grid=(B, L // T),
        in_specs=[
            pl.BlockSpec((1, T, COL_TILE), lambda b, i: (b, i, 0)),
            pl.BlockSpec((1, HIST_ROWS, POOL_WIDTH), lambda b, i: (b, 0, 0)),
            pl.BlockSpec((1, SUBLANES, 2 * S5_FLAT), lambda b, i: (b, 0, 0)),
            pl.BlockSpec((len(POOL_WINDOWS), POOL_GROUP, POOL_GROUP), lambda b, i: (0, 0, 0)),
            pl.BlockSpec((1, POOL_WIDTH), const2),
            pl.BlockSpec((1, S5_TABLES, SUBLANES, S5_FLAT), const4),
            pl.BlockSpec((2, S5_BLOCKS, S5_BLOCK_CH, S5_LANE_CHUNK), const4),
            pl.BlockSpec((2, S5_BLOCKS, S5_LANE_CHUNK, S5_BLOCK_CH), const4),
            pl.BlockSpec((1, S5_WIDTH), const2),
            pl.BlockSpec((S5_WIDTH, S5_WIDTH), const2),
        ],
        out_specs=[
            pl.BlockSpec((1, T, COL_TILE), lambda b, i: (b, i, 0)),
            pl.BlockSpec((1, HIST_ROWS, POOL_WIDTH), lambda b, i: (b, 0, 0)),
            pl.BlockSpec((1, 1, 2 * S5_FLAT), lambda b, i: (b, 0, 0)),
        ],
        out_shape=[
            jax.ShapeDtypeStruct((B, L, COL_TILE), BF16),
            jax.ShapeDtypeStruct((B, HIST_ROWS, POOL_WIDTH), F32),
            jax.ShapeDtypeStruct((B, 1, 2 * S5_FLAT), F32),
        ],
        scratch_shapes=[
            pltpu.VMEM((HIST_ROWS + T, POOL_WIDTH), F32),
            pltpu.VMEM((T, 2 * S5_FLAT), F32),
            pltpu.VMEM((SUBLANES, 2 * S5_FLAT), F32),
        ],
        compiler_params=_params("arbitrary", "arbitrary"),
        name="pools5",
    )(pu, hist, h0, w_pool, pool_scale, tab, b_blk, c_blk, d_skip, w_glu)


def _tri_ones(n):
    j = lax.broadcasted_iota(jnp.int32, (2 * n, 2 * n), 0)
    j = jnp.where(j >= n, j - n, j)
    s = lax.broadcasted_iota(jnp.int32, (2 * n, 2 * n), 1)
    return jnp.where((j >= s) | (s >= n), 1.0, 0.0).astype(BF16)


def _scores(q, k):
    return lax.dot_general(q, k, (((1,), (1,)), ((), ())),
                           preferred_element_type=F32) * (SB_HEAD_DIM ** -0.5 * LOG2_E)


def _stick_blocks(scores, vs, mask, laters, tri, stored_scores=False, after=None):
    n = vs[0].shape[0]
    n_pairs = len(vs)
    zs, sums, results = {}, {}, []
    for step in range(n_pairs + 2 * STAGE_SKEW):
        a, b, c = step, step - STAGE_SKEW, step - 2 * STAGE_SKEW
        if a < n_pairs and not stored_scores:
            zs[a] = scores[a]()
        if 0 <= b < n_pairs:
            z = scores[b]() if stored_scores else zs[b]
            sp = jnp.maximum(z, 0.0) + jnp.log2(1.0 + jnp.exp2(-jnp.abs(z)))
            if mask is not None:
                sp = jnp.where(mask, sp, 0.0)
            sp_hi = sp.astype(BF16)
            sp_lo = (sp - sp_hi.astype(F32)).astype(BF16)
            sums[b] = jnp.dot(jnp.concatenate([sp_hi, sp_lo], axis=1), tri, preferred_element_type=F32)
        if 0 <= c < n_pairs:
            s, later = sums.pop(c), laters[c]
            z = scores[c]() if stored_scores else zs.pop(c)
            w = jnp.exp2(z + (later - s[:, 0:n]))
            if mask is not None:
                w = jnp.where(mask, w, 0.0)
            results.append((jnp.dot(w.astype(BF16), vs[c], preferred_element_type=F32), later - s[:, n:2 * n]))
            if after is not None:
                after[c]()
    return results


def _attn_prompt_kernel(q_ref, k_ref, v_ref, o_ref, acc_ref, later_ref, z_ref):
    tile = pl.program_id(2)
    blk = ATT_BLOCK
    n_chain = q_ref.shape[2] // blk
    chains = range(n_chain)
    tri = _tri_ones(blk)
    row = lax.broadcasted_iota(jnp.int32, (blk, blk), 0)
    col = lax.broadcasted_iota(jnp.int32, (blk, blk), 1)
    causal = col < row
    zeros = jnp.zeros((blk, blk), F32)

    def key_rows(kb):
        return pl.ds(pl.multiple_of(jnp.maximum(kb, 0) * blk, blk), blk)

    def q_of(c):
        return q_ref[0, 0, c * blk:(c + 1) * blk, :]

    def score_ahead(c, kb):
        def run():
            z_ref[c] = _scores(q_of(c), k_ref[0, 0, key_rows(kb), :])
        return run

    def stored(c):
        return lambda: z_ref[c]

    diag = [tile * n_chain + c for c in chains]
    results = _stick_blocks([lambda c=c: _scores(q_of(c), k_ref[0, 0, key_rows(diag[c]), :]) for c in chains],
                            [v_ref[0, 0, key_rows(diag[c]), :] for c in chains], causal,
                            [zeros] * n_chain, tri, after=[score_ahead(c, diag[c] - 1) for c in chains])
    alive = None
    for c, (out, later) in enumerate(results):
        acc_ref[c] = out
        later_ref[c] = later
        alive = later if alive is None else jnp.maximum(alive, later)

    def cond(carry):
        t, go = carry
        return jnp.logical_and(t <= tile * n_chain + (n_chain - 1), go > 0)

    def body(carry):
        t, _ = carry
        kbs = [diag[c] - t for c in chains]
        laters = [jnp.where(kbs[c] < 0, STICK_PARKED, later_ref[c]) for c in chains]
        results = _stick_blocks([stored(c) for c in chains], [v_ref[0, 0, key_rows(kbs[c]), :] for c in chains],
                                None, laters, tri, stored_scores=True,
                                after=[score_ahead(c, kbs[c] - 1) for c in chains])
        alive = None
        for c, (out, later) in enumerate(results):
            acc_ref[c] += out
            later_ref[c] = later
            alive = later if alive is None else jnp.maximum(alive, later)
        return t + 1, (jnp.max(alive) >= STICK_LOG2_FLOOR).astype(jnp.int32)

    go = (jnp.max(alive) >= STICK_LOG2_FLOOR).astype(jnp.int32)
    lax.while_loop(cond, body, (jnp.int32(1), go))
    for c in chains:
        o_ref[0, c * blk:(c + 1) * blk, :] = acc_ref[c].astype(BF16)


def _attn_prompt(proj_bf):
    _, B, L, _ = proj_bf.shape
    tq = min(ATT_CHAINS * ATT_BLOCK, L)
    kv_spec = lambda part: pl.BlockSpec((1, 1, L, SB_HEAD_DIM), lambda b, h, i: (part, b, 0, h))
    return pl.pallas_call(
        _attn_prompt_kernel,
        grid=(B, SB_HEADS, L // tq),
        in_specs=[
            pl.BlockSpec((1, 1, tq, SB_HEAD_DIM), lambda b, h, i: (0, b, i, h)),
            kv_spec(1),
            kv_spec(2),
        ],
        out_specs=pl.BlockSpec((1, tq, SB_HEAD_DIM), lambda b, h, i: (b, i, h)),
        out_shape=jax.ShapeDtypeStruct((B, L, SB_WIDTH), BF16),
        scratch_shapes=[
            pltpu.VMEM((tq // ATT_BLOCK, ATT_BLOCK, ATT_BLOCK), F32),
            pltpu.VMEM((tq // ATT_BLOCK, ATT_BLOCK, ATT_BLOCK), F32),
            pltpu.VMEM((tq // ATT_BLOCK, ATT_BLOCK, ATT_BLOCK), F32),
        ],
        compiler_params=_params("arbitrary", "arbitrary", "arbitrary"),
        name="attn_prompt",
    )(proj_bf, proj_bf, proj_bf)


def _attn_sample_kernel(q_ref, kn_ref, vn_ref, kc_ref, vc_ref, o_ref):
    L = q_ref.shape[2]
    past = kc_ref.shape[2]
    cblk = min(ATT_BLOCK, past)
    tri_new = _tri_ones(L)
    tri_c = _tri_ones(cblk)
    row = lax.broadcasted_iota(jnp.int32, (L, L), 0)
    col = lax.broadcasted_iota(jnp.int32, (L, L), 1)
    causal = col < row
    lanes = [slice(h * SB_HEAD_DIM, (h + 1) * SB_HEAD_DIM) for h in range(SB_HEADS)]
    qs = [q_ref[0, 0, :, ln] for ln in lanes]
    results = _stick_blocks([lambda q=q, ln=ln: _scores(q, kn_ref[0, 0, :, ln]) for q, ln in zip(qs, lanes)],
                            [vn_ref[0, 0, :, ln] for ln in lanes], causal,
                            [jnp.zeros((L, L), F32)] * SB_HEADS, tri_new)
    accs = [out for out, _ in results]
    laters = [jnp.broadcast_to(later[:, 0:1], (L, cblk)) for _, later in results]
    for j in range(past // cblk - 1, -1, -1):
        rows = slice(j * cblk, (j + 1) * cblk)
        kc = pltpu.einshape("phd->p(hd)", kc_ref[0, 0, rows, :, :]).astype(BF16)
        vc = pltpu.einshape("phd->p(hd)", vc_ref[0, 0, rows, :, :]).astype(BF16)
        results = _stick_blocks([lambda q=q, ln=ln, kc=kc: _scores(q, kc[:, ln]) for q, ln in zip(qs, lanes)],
                                [vc[:, ln] for ln in lanes], None, laters, tri_c)
        accs = [acc + out for acc, (out, _) in zip(accs, results)]
        laters = [later for _, later in results]
    for ln, acc in zip(lanes, accs):
        o_ref[0, :, ln] = acc.astype(BF16)


def _attn_sample(proj_bf, cache_k, cache_v, layer):
    _, B, L, _ = proj_bf.shape
    past = cache_k.shape[2]
    new_spec = lambda part: pl.BlockSpec((1, 1, L, SB_WIDTH), lambda b: (part, b, 0, 0))
    cache_spec = pl.BlockSpec((1, 1, past, SB_HEADS, SB_HEAD_DIM), lambda b: (layer, b, 0, 0, 0))
    return pl.pallas_call(
        _attn_sample_kernel,
        grid=(B,),
        in_specs=[new_spec(0), new_spec(1), new_spec(2), cache_spec, cache_spec],
        out_specs=pl.BlockSpec((1, L, SB_WIDTH), lambda b: (b, 0, 0)),
        out_shape=jax.ShapeDtypeStruct((B, L, SB_WIDTH), BF16),
        compiler_params=_params("arbitrary"),
        name="attn_sample",
    )(proj_bf, proj_bf, proj_bf, cache_k, cache_v)


def _outproj_kernel(mix_ref, att_ref, x_ref, g_ref, sc_ref, sh_ref, w_ref, lng_ref, lnb_ref, o_ref, h_ref):
    bb, tl, _ = x_ref.shape
    half = mix_ref.shape[-1]
    for bs, ts in _row_chunks(bb, tl):
        m = mix_ref[bs, ts, :]
        cb, ct, _ = m.shape
        mix = (jnp.dot(m.reshape(cb * ct, half), w_ref[0:half, :], preferred_element_type=F32)
               + jnp.dot(att_ref[bs, ts, :].reshape(cb * ct, half), w_ref[half:2 * half, :],
                         preferred_element_type=F32))
        t = DEEPNORM_ALPHA * x_ref[bs, ts, :] + g_ref[bs] * mix.reshape(cb, ct, D_MODEL)
        x1 = _layer_norm(t) * lng_ref[...] + lnb_ref[...]
        o_ref[bs, ts, :] = x1
        h_ref[bs, ts, :] = (_layer_norm(x1) * (1.0 + sc_ref[bs]) + sh_ref[bs]).astype(BF16)


def _outproj(mix, att, x, gate, sc2, sh2, w_out, layer, ln_g, ln_b, bb, tl):
    B, L, _ = x.shape
    tok = lambda width: pl.BlockSpec((bb, tl, width), lambda b, i: (b, i, 0))
    mod = pl.BlockSpec((bb, 1, D_MODEL), lambda b, i: (b, 0, 0))
    vec = pl.BlockSpec((1, 1, D_MODEL), lambda b, i: (0, 0, 0))
    return pl.pallas_call(
        _outproj_kernel,
        grid=(B // bb, L // tl),
        in_specs=[
            tok(COL_TILE), tok(SB_WIDTH), tok(D_MODEL), mod, mod, mod,
            pl.BlockSpec((None, 2 * COL_TILE, D_MODEL), lambda b, i: (layer, 0, 0),
                         pipeline_mode=pl.Buffered(1)),
            vec, vec,
        ],
        out_specs=[tok(D_MODEL), tok(D_MODEL)],
        out_shape=[jax.ShapeDtypeStruct((B, L, D_MODEL), F32), jax.ShapeDtypeStruct((B, L, D_MODEL), BF16)],
        compiler_params=_params("arbitrary", "arbitrary"),
        name="outproj",
    )(mix, att, x, gate, sc2, sh2, w_out, ln_g.reshape(1, 1, D_MODEL), ln_b.reshape(1, 1, D_MODEL))


def _ffn_kernel(h_ref, x_ref, g_ref, wup_ref, wdn_ref, lng_ref, lnb_ref, o_ref, acc_ref):
    f = pl.program_id(2)
    bb, tl, _ = x_ref.shape
    rows = bb * tl

    def partial_sum():
        hid = jnp.dot(h_ref[...].reshape(rows, D_MODEL), wup_ref[...], preferred_element_type=F32)
        hid = jnp.square(jnp.maximum(hid, 0.0))
        return jnp.dot(hid.astype(BF16), wdn_ref[...], preferred_element_type=F32)

    @pl.when(f == 0)
    def _():
        acc_ref[...] = partial_sum()

    @pl.when(f > 0)
    def _():
        acc_ref[...] += partial_sum()

    @pl.when(f == pl.num_programs(2) - 1)
    def _():
        t = DEEPNORM_ALPHA * x_ref[...] + g_ref[...] * acc_ref[...].reshape(bb, tl, D_MODEL)
        o_ref[...] = _layer_norm(t) * lng_ref[...] + lnb_ref[...]


def _ffn(h2, x, gate, w_up, w_down, layer, ln_g, ln_b, bb, tl):
    B, L, _ = x.shape
    tf = FF_TILE
    tok = pl.BlockSpec((bb, tl, D_MODEL), lambda b, i, f: (b, i, 0))
    mod = pl.BlockSpec((bb, 1, D_MODEL), lambda b, i, f: (b, 0, 0))
    vec = pl.BlockSpec((1, 1, D_MODEL), lambda b, i, f: (0, 0, 0))
    return pl.pallas_call(
        _ffn_kernel,
        grid=(B // bb, L // tl, D_FF // tf),
        in_specs=[
            tok, tok, mod,
            pl.BlockSpec((None, D_MODEL, tf), lambda b, i, f: (layer, 0, f)),
            pl.BlockSpec((None, tf, D_MODEL), lambda b, i, f: (layer, f, 0)),
            vec, vec,
        ],
        out_specs=tok,
        out_shape=jax.ShapeDtypeStruct((B, L, D_MODEL), F32),
        scratch_shapes=[pltpu.VMEM((bb * tl, D_MODEL), F32)],
        compiler_params=_params("arbitrary", "arbitrary", "arbitrary"),
        name="ffn",
    )(h2, x, gate, w_up, w_down, ln_g.reshape(1, 1, D_MODEL), ln_b.reshape(1, 1, D_MODEL))


def _row_blocking(B, L):
    tl = min(L, ROW_TILE)
    bb = max(1, min(B, ROW_TILE // tl))
    return bb, tl


def _run_group(x, ada, hist, s5_re, s5_im, cache_k, cache_v, pos0, lw, s5_raw):
    B, L, _ = x.shape
    bb, tl = _row_blocking(B, L)
    chunk = min(L, S5_CHUNK)
    tab, _ = _s5prep(*s5_raw, chunk // SUBLANES)
    new_hist, new_re, new_im = [], [], []
    kv = None
    for l in range(DEPTH):
        w = lw[l]
        sh1, sc1, g1, sh2, sc2, g2 = [ada[l][:, None, n * D_MODEL:(n + 1) * D_MODEL] for n in range(6)]
        pu, k_all, v_all, proj_bf = _inproj(x, sc1, sh1, w["w_in"], kv, l, bb, tl)
        kv = (k_all, v_all)
        hist16 = jnp.pad(hist[l], ((0, 0), (HIST_ROWS - POOL_HIST, 0), (0, 0)))
        h0 = jnp.concatenate([s5_re[l].reshape(B, 1, S5_FLAT), s5_im[l].reshape(B, 1, S5_FLAT)], axis=-1)
        h0 = jnp.broadcast_to(h0, (B, SUBLANES, 2 * S5_FLAT))
        mix, nh, ns = _pools5(pu, hist16, h0, w["w_pool"], w["pool_scale"], tab[l:l + 1], w["b_blk"],
                              w["c_blk"], w["d_skip"], w["w_glu"], pos0, chunk)
        if cache_k is None:
            att = _attn_prompt(proj_bf)
        else:
            att = _attn_sample(proj_bf, cache_k, cache_v, l)
        x, h2 = _outproj(mix, att, x, g1, sc2, sh2, w["w_out"], l, w["ln1_g"], w["ln1_b"], bb, tl)
        x = _ffn(h2, x, g2, w["w_up"], w["w_down"], l, w["ln2_g"], w["ln2_b"], bb, tl)
        new_hist.append(nh[:, HIST_ROWS - POOL_HIST:, :])
        new_re.append(ns[:, 0, 0:S5_FLAT].reshape(B, S5_GROUPS, S5_STATE))
        new_im.append(ns[:, 0, S5_FLAT:].reshape(B, S5_GROUPS, S5_STATE))
    stack = lambda xs: jnp.stack(xs)
    return x, (stack(new_hist), stack(new_re), stack(new_im), kv[0], kv[1])


def kernel(x_prompt, x_sample, state_pool, state_s5_re, state_s5_im, cache_k, cache_v, c_prompt, c_sample,
           w_ada, b_ada, w_in, w_pool, pool_scale, s5_a_re, s5_a_im, s5_log_dt, s5_b_re, s5_b_im,
           s5_c_re, s5_c_im, s5_d, w_glu, w_out, ln1_g, ln1_b, w_up, w_down, ln2_g, ln2_b):
    B = x_prompt.shape[0]
    Bs = x_sample.shape[0]
    past = cache_k.shape[2]

    n_c = B + Bs
    pad = (-n_c) % 16
    c_all = jnp.concatenate([c_prompt, c_sample, jnp.zeros((pad, D_MODEL), F32)], axis=0)
    ada = _ada(c_all, w_ada, b_ada)
    ada_p, ada_s = ada[:, 0:B], ada[:, B:n_c]

    s5_raw = (s5_a_re, s5_a_im, s5_log_dt, s5_b_re, s5_b_im)
    _, bb_disc = _s5prep(*s5_raw, SUBLANES)
    w_in_bf, w_out_bf, w_up_bf, w_down_bf = [_to_bf16(w) for w in (w_in, w_out, w_up, w_down)]
    lw = []
    for l in range(DEPTH):
        b_blk, c_blk = _s5_block_weights(bb_disc[l], s5_c_re[l], s5_c_im[l])
        lw.append(dict(
            w_in=w_in_bf, w_out=w_out_bf, w_up=w_up_bf, w_down=w_down_bf,
            w_glu=w_glu[l].astype(BF16), w_pool=w_pool[l].astype(BF16),
            pool_scale=pool_scale[l].reshape(1, POOL_WIDTH),
            d_skip=s5_d[l].reshape(1, S5_WIDTH),
            b_blk=b_blk, c_blk=c_blk,
            ln1_g=ln1_g[l], ln1_b=ln1_b[l], ln2_g=ln2_g[l], ln2_b=ln2_b[l],
        ))

    zero_hist = jnp.zeros((DEPTH, B, POOL_HIST, POOL_WIDTH), F32)
    zero_s5 = jnp.zeros((DEPTH, B, S5_GROUPS, S5_STATE), F32)
    y_p, (pool_p, re_p, im_p, k_p, v_p) = _run_group(
        x_prompt, ada_p, zero_hist, zero_s5, zero_s5, None, None, 0, lw, s5_raw)
    y_s, (pool_s, re_s, im_s, k_s, v_s) = _run_group(
        x_sample, ada_s, state_pool, state_s5_re, state_s5_im, cache_k, cache_v, past, lw, s5_raw)
    return (y_p, y_s, pool_p, re_p, im_p, k_p, v_p, pool_s, re_s, im_s, k_s, v_s)
```
